```python
import jax, jax.numpy as jnp
from jax import lax
import numpy as np

D_MODEL = 1024
BATCH = 16
SEQ = 2048
DEPTH = 4

HEAD_DIM = 64
ATT_WIDTH = D_MODEL // 2
ML_WIDTH = D_MODEL // 4
GM_WIDTH = D_MODEL // 4
MIX_WIDTH = ATT_WIDTH + ML_WIDTH + GM_WIDTH
N_ATT_HEADS = ATT_WIDTH // HEAD_DIM
N_KV_HEADS = max(1, N_ATT_HEADS // 4)
KV_WIDTH = N_KV_HEADS * HEAD_DIM
N_ML_HEADS = ML_WIDTH // HEAD_DIM
N_GM_GROUPS = GM_WIDTH // HEAD_DIM
WINDOW = 128
ATT_BLOCK = 128
N_REL_BUCKETS = 32
REL_MAX_DIST = 128
ML_CHUNK = 128
ML_CONV = 3
N_ML_GATES = 4 * N_ML_HEADS
GM_CHUNK = 128
N_EXPERTS = 16
EC_CAPACITY_FACTOR = 2
D_EXPERT = 2 * D_MODEL
DN_ALPHA = (2 * DEPTH) ** 0.25
DN_BETA = (8 * DEPTH) ** -0.25
LN_EPS = 1e-5
NEG_INF = -1e30
PROJ_SPLITS = (ATT_WIDTH, KV_WIDTH, KV_WIDTH, ML_WIDTH, ML_WIDTH, ML_WIDTH, ML_WIDTH, N_ML_GATES, GM_WIDTH, GM_WIDTH)
PROJ_WIDTH = ATT_WIDTH + 2 * KV_WIDTH + 4 * ML_WIDTH + N_ML_GATES + 2 * GM_WIDTH

kernel_name = "hybrid_parallel_groups_ec_moe_encoder"


def layer_norm(x, g=None, b=None):
    xf = x.astype(jnp.float32)
    mu = xf.mean(-1, keepdims=True)
    var = jnp.mean(jnp.square(xf - mu), -1, keepdims=True)
    y = (xf - mu) * lax.rsqrt(var + LN_EPS)
    if g is not None:
        y = y * g.astype(jnp.float32) + b.astype(jnp.float32)
    return y.astype(x.dtype)


def t5_buckets(rel):
    nb = N_REL_BUCKETS // 2
    max_exact = nb // 2
    ret = np.where(rel > 0, nb, 0)
    n = np.abs(rel)
    large = max_exact + (np.log(np.maximum(n, 1) / max_exact) / np.log(REL_MAX_DIST / max_exact) * (nb - max_exact)).astype(np.int32)
    large = np.minimum(large, nb - 1)
    return (ret + np.where(n < max_exact, n, large)).astype(np.int32)


def windowed_gqa(q, k, v, sink, rel_bias):
    B, S, Hq, dh = q.shape
    Hkv = k.shape[2]
    G = Hq // Hkv
    nb = S // ATT_BLOCK
    band = 3 * ATT_BLOCK

    def to_band(t):
        tp = jnp.pad(t, ((0, 0), (ATT_BLOCK, ATT_BLOCK), (0, 0), (0, 0)))
        blocks = tp.reshape(B, nb + 2, ATT_BLOCK, Hkv, dh)
        return jnp.concatenate([blocks[:, :-2], blocks[:, 1:-1], blocks[:, 2:]], axis=2)

    kb, vb = to_band(k), to_band(v)
    qb = q.reshape(B, nb, ATT_BLOCK, Hkv, G, dh)
    logits = jnp.einsum('bnqhgd,bnshd->bhgnqs', qb, kb, preferred_element_type=jnp.float32) * (dh ** -0.5)
    qi = np.arange(ATT_BLOCK)[:, None]
    sj = np.arange(band)[None, :]
    rel = sj - ATT_BLOCK - qi
    bias = rel_bias[t5_buckets(rel)].astype(jnp.float32)
    bias = bias.transpose(2, 0, 1).reshape(Hkv, G, 1, ATT_BLOCK, band)
    kpos = np.arange(nb)[:, None] * ATT_BLOCK - ATT_BLOCK + sj
    mask = (np.abs(rel) <= WINDOW)[None] & ((kpos >= 0) & (kpos < S))[:, None, :]
    logits = jnp.where(mask, logits + bias, NEG_INF)
    sink_col = jnp.broadcast_to(sink.astype(jnp.float32).reshape(Hkv, G, 1, 1, 1), logits.shape[:-1] + (1,))
    probs = jax.nn.softmax(jnp.concatenate([logits, sink_col], axis=-1), axis=-1)[..., :-1]
    out = jnp.einsum('bhgnqs,bnshd->bnqhgd', probs.astype(vb.dtype), vb)
    return out.reshape(B, S, Hq * dh)


def short_conv(t, w, b):
    C = t.shape[-1]
    y = lax.conv_general_dilated(t, w[:, None, :], window_strides=(1,), padding=[(ML_CONV // 2, ML_CONV // 2)],
                                 dimension_numbers=('NWC', 'WIO', 'NWC'), feature_group_count=C)
    return y + b


def mlstm_direction(q, k, v, ig, lf):
    B, H, S, dh = q.shape
    nc = S // ML_CHUNK

    def chunks(t):
        return jnp.moveaxis(t.reshape((B, H, nc, ML_CHUNK) + t.shape[3:]), 2, 0)

    tril = jnp.tril(jnp.ones((ML_CHUNK, ML_CHUNK), dtype=bool))

    def step(carry, inp):
        C, n, m = carry
        qc, kc, vc, ic, fc = inp
        b = jnp.cumsum(fc, axis=-1)
        dmat = jnp.where(tril, b[..., :, None] - b[..., None, :] + ic[..., None, :], -jnp.inf)
        inter = b + m[..., None]
        m_t = jnp.maximum(inter, dmat.max(-1))
        s = jnp.einsum('bhtd,bhsd->bhts', qc, kc) * jnp.exp(dmat - m_t[..., None])
        iw = jnp.exp(inter - m_t)
        num = jnp.einsum('bhts,bhsd->bhtd', s, vc) + iw[..., None] * jnp.einsum('bhed,bhtd->bhte', C, qc)
        den = s.sum(-1) + iw * jnp.einsum('bhd,bhtd->bht', n, qc)
        h = num / jnp.maximum(jnp.abs(den), jnp.exp(-m_t))[..., None]
        bL = b[..., -1]
        ws = bL[..., None] - b + ic
        m_new = jnp.maximum(bL + m, ws.max(-1))
        wexp = jnp.exp(ws - m_new[..., None])
        decay = jnp.exp(bL + m - m_new)
        C_new = decay[..., None, None] * C + jnp.einsum('bhs,bhse,bhsd->bhed', wexp, vc, kc)
        n_new = decay[..., None] * n + jnp.einsum('bhs,bhsd->bhd', wexp, kc)
        return (C_new, n_new, m_new), h

    init = (jnp.zeros((B, H, dh, dh), jnp.float32), jnp.zeros((B, H, dh), jnp.float32), jnp.zeros((B, H), jnp.float32))
    _, h = lax.scan(step, init, (chunks(q), chunks(k), chunks(v), chunks(ig), chunks(lf)))
    return jnp.moveaxis(h, 0, 2).reshape(B, H, S, dh)


def mlstm_bidir(mq, mk, mv, mo, gates):
    B, S, _ = mq.shape
    H = N_ML_HEADS

    def heads(t):
        return t.astype(jnp.float32).reshape(B, S, H, HEAD_DIM).transpose(0, 2, 1, 3)

    q, k, v = heads(mq), heads(mk) * (HEAD_DIM ** -0.5), heads(mv)
    g = gates.astype(jnp.float32).reshape(B, S, 4, H).transpose(2, 0, 3, 1)
    flip = lambda t: jnp.flip(t, axis=2)
    h_f = mlstm_direction(q, k, v, g[0], jax.nn.log_sigmoid(g[1]))
    h_b = flip(mlstm_direction(flip(q), flip(k), flip(v), flip(g[2]), flip(jax.nn.log_sigmoid(g[3]))))
    h = (h_f + h_b).transpose(0, 2, 1, 3).reshape(B, S, ML_WIDTH)
    return jax.nn.sigmoid(mo) * h.astype(mo.dtype)


def spatial_gating(u, v, w_s, b_s):
    B, S, _ = u.shape
    u = jax.nn.gelu(u, approximate=False)
    v = layer_norm(jax.nn.gelu(v, approximate=False))
    nc = S // GM_CHUNK
    vb = v.reshape(B, nc, GM_CHUNK, N_GM_GROUPS, HEAD_DIM)
    s = jnp.einsum('gts,bnsgc->bntgc', w_s, vb) + b_s.T[:, :, None]
    return u * s.reshape(B, S, GM_WIDTH)


def expert_choice_ffn(h, w_router, w_gate, w_up, w_down):
    B, S, D = h.shape
    cap = max(1, min(S, EC_CAPACITY_FACTOR * S // N_EXPERTS))
    aff = jax.nn.softmax(jnp.einsum('bsd,de->bse', h, w_router, preferred_element_type=jnp.float32), axis=-1)
    gate, idx = lax.top_k(aff.transpose(0, 2, 1), cap)
    xg = jax.vmap(lambda hb, ib: hb[ib])(h, idx)
    hid = jax.nn.silu(jnp.einsum('becd,edf->becf', xg, w_gate)) * jnp.einsum('becd,edf->becf', xg, w_up)
    ye = jnp.einsum('becf,efd->becd', hid, w_down) * gate.astype(h.dtype)[..., None]
    scatter = lambda yb, ib: jnp.zeros((S, D), yb.dtype).at[ib.reshape(-1)].add(yb.reshape(-1, D))
    return jax.vmap(scatter)(ye, idx)


def hybrid_layer(x, c, w_ada, b_ada, w_in, conv_w, conv_b, gate_b, sink, rel_bias, w_s, b_s, w_out,
                 w_router, w_gate, w_up, w_down, ln_g, ln_b):
    B, S, D = x.shape
    mod = jax.nn.silu(c) @ w_ada + b_ada
    sh1, sc1, g1, sh2, sc2, g2 = [m[:, None, :] for m in jnp.split(mod, 6, axis=-1)]
    h = layer_norm(x) * (1 + sc1) + sh1
    proj = h @ w_in
    aq, ak, av, mq, mk, mv, mo, mg, gu, gv = jnp.split(proj, np.cumsum(PROJ_SPLITS)[:-1].tolist(), axis=-1)
    att = windowed_gqa(aq.reshape(B, S, N_ATT_HEADS, HEAD_DIM), ak.reshape(B, S, N_KV_HEADS, HEAD_DIM),
                       av.reshape(B, S, N_KV_HEADS, HEAD_DIM), sink, rel_bias)
    mqk = jax.nn.silu(short_conv(jnp.concatenate([mq, mk], axis=-1), conv_w, conv_b))
    mq, mk = jnp.split(mqk, 2, axis=-1)
    ml = mlstm_bidir(mq, mk, mv, mo, mg + gate_b)
    gm = spatial_gating(gu, gv, w_s, b_s)
    mix = jnp.concatenate([att, ml, gm], axis=-1) @ w_out
    x = layer_norm(DN_ALPHA * x + (1 + g1) * mix, ln_g[0], ln_b[0])
    h2 = layer_norm(x) * (1 + sc2) + sh2
    y = expert_choice_ffn(h2, w_router, w_gate, w_up, w_down)
    return layer_norm(DN_ALPHA * x + (1 + g2) * y, ln_g[1], ln_b[1])


def setup_inputs(seed: int = 0) -> dict:
    key = jax.random.key(seed)
    ks = jax.random.split(key, 20)
    nrm = lambda k, shape, scale: jax.random.normal(k, shape, jnp.float32) * scale
    D = D_MODEL
    lin = jnp.linspace(3.0, 6.0, N_ML_HEADS, dtype=jnp.float32)
    zer = jnp.zeros((N_ML_HEADS,), jnp.float32)
    gate_base = jnp.concatenate([zer, lin, zer, lin])
    return {
        'x': nrm(ks[0], (BATCH, SEQ, D), 1.0),
        'c': nrm(ks[1], (BATCH, D), 1.0),
        'w_ada': nrm(ks[2], (DEPTH, D, 6 * D), 0.2 * D ** -0.5),
        'b_ada': nrm(ks[3], (DEPTH, 6 * D), 0.02),
        'w_in': nrm(ks[4], (DEPTH, D, PROJ_WIDTH), D ** -0.5),
        'conv_w': nrm(ks[5], (DEPTH, ML_CONV, 2 * ML_WIDTH), ML_CONV ** -0.5),
        'conv_b': nrm(ks[6], (DEPTH, 2 * ML_WIDTH), 0.02),
        'gate_b': gate_base + nrm(ks[7], (DEPTH, N_ML_GATES), 0.1),
        'sink': nrm(ks[8], (DEPTH, N_ATT_HEADS), 0.5),
        'rel_bias': nrm(ks[9], (N_REL_BUCKETS, N_ATT_HEADS), 0.1),
        'w_s': nrm(ks[10], (DEPTH, N_GM_GROUPS, GM_CHUNK, GM_CHUNK), GM_CHUNK ** -0.5),
        'b_s': 1.0 + nrm(ks[11], (DEPTH, N_GM_GROUPS, GM_CHUNK), 0.02),
        'w_out': nrm(ks[12], (DEPTH, MIX_WIDTH, D), DN_BETA * MIX_WIDTH ** -0.5),
        'w_router': nrm(ks[13], (DEPTH, D, N_EXPERTS), D ** -0.5),
        'w_gate': nrm(ks[14], (DEPTH, N_EXPERTS, D, D_EXPERT), D ** -0.5),
        'w_up': nrm(ks[15], (DEPTH, N_EXPERTS, D, D_EXPERT), D ** -0.5),
        'w_down': nrm(ks[16], (DEPTH, N_EXPERTS, D_EXPERT, D), DN_BETA * D_EXPERT ** -0.5),
        'ln_g': 1.0 + nrm(ks[17], (DEPTH, 2, D), 0.02),
        'ln_b': nrm(ks[18], (DEPTH, 2, D), 0.02),
    }


def reference(x, c, w_ada, b_ada, w_in, conv_w, conv_b, gate_b, sink, rel_bias, w_s, b_s, w_out,
              w_router, w_gate, w_up, w_down, ln_g, ln_b):
    for l in range(DEPTH):
        x = hybrid_layer(x, c, w_ada[l], b_ada[l], w_in[l], conv_w[l], conv_b[l], gate_b[l], sink[l], rel_bias,
                         w_s[l], b_s[l], w_out[l], w_router[l], w_gate[l], w_up[l], w_down[l], ln_g[l], ln_b[l])
    return x
```

```python
import functools

import numpy as np
import jax
import jax.numpy as jnp
from jax import lax
from jax.experimental import pallas as pl
from jax.experimental.pallas import tpu as pltpu

F32 = jnp.float32
BF16 = jnp.bfloat16
HIGHEST = lax.Precision.HIGHEST

HEAD_DIM = 64
LANES = 128
CHUNK = 128
N_ATT_HEADS = 8
N_KV_HEADS = 2
N_ML_HEADS = 4
N_GM_GROUPS = 4
ML_CONV = 3
N_REL_BUCKETS = 32
REL_MAX_DIST = 128
WINDOW = 128
EC_CAPACITY_FACTOR = 2
LN_EPS = 1e-5
NEG_INF = -1e30
VMEM_LIMIT = 56 * 1024 * 1024

NT_DIMS = (((1,), (1,)), ((), ()))
TN_DIMS = (((0,), (0,)), ((), ()))


def _cparams(*sem):
    return pltpu.CompilerParams(dimension_semantics=sem, vmem_limit_bytes=VMEM_LIMIT)


def _ln(x):
    mu = jnp.mean(x, axis=-1, keepdims=True)
    xc = x - mu
    var = jnp.mean(xc * xc, axis=-1, keepdims=True)
    return xc * lax.rsqrt(var + LN_EPS)


def _gelu(x):
    return 0.5 * x * (1.0 + lax.erf(x * (2.0 ** -0.5)))


def _silu(x):
    return x * jax.nn.sigmoid(x)


def _mod_kernel(c_ref, w_ref, b_ref, o_ref):
    sc = _silu(c_ref[...])
    o_ref[0] = jnp.dot(sc, w_ref[0], precision=HIGHEST, preferred_element_type=F32) + b_ref[0]


def _modulation(c, w_ada, b_ada):
    depth, d, n = w_ada.shape
    b = c.shape[0]
    tn = 1536
    return pl.pallas_call(
        _mod_kernel,
        grid=(depth, n // tn),
        in_specs=[pl.BlockSpec((b, d), lambda l, j: (0, 0)),
                  pl.BlockSpec((1, d, tn), lambda l, j: (l, 0, j)),
                  pl.BlockSpec((1, 1, tn), lambda l, j: (l, 0, j))],
        out_specs=pl.BlockSpec((1, b, tn), lambda l, j: (l, 0, j)),
        out_shape=jax.ShapeDtypeStruct((depth, b, n), F32),
        compiler_params=_cparams("parallel", "parallel"),
        name="modulation",
    )(c, w_ada, b_ada.reshape(depth, 1, n))


def _t5_buckets(rel):
    nb = N_REL_BUCKETS // 2
    max_exact = nb // 2
    ret = np.where(rel > 0, nb, 0)
    n = np.abs(rel)
    large = max_exact + (np.log(np.maximum(n, 1) / max_exact) / np.log(REL_MAX_DIST / max_exact)
                         * (nb - max_exact)).astype(np.int32)
    large = np.minimum(large, nb - 1)
    return (ret + np.where(n < max_exact, n, large)).astype(np.int32)


def _bias_kernel(rb_ref, bk_ref, mk_ref, o_ref):
    bk = bk_ref[...]
    inband = mk_ref[...] != 0
    for h in range(N_ATT_HEADS):
        acc = jnp.zeros(bk.shape, F32)
        for k in range(N_REL_BUCKETS):
            acc = jnp.where(bk == k, rb_ref[k, h], acc)
        o_ref[h] = jnp.where(inband, acc, NEG_INF)


def _band_bias(rel_bias):
    qi = np.arange(CHUNK)[:, None]
    sj = np.arange(3 * CHUNK)[None, :]
    rel = sj - CHUNK - qi
    buckets = jnp.asarray(_t5_buckets(rel), jnp.int32)
    inband = jnp.asarray((np.abs(rel) <= WINDOW).astype(np.int32))
    return pl.pallas_call(
        _bias_kernel,
        in_specs=[pl.BlockSpec(memory_space=pltpu.SMEM),
                  pl.BlockSpec((CHUNK, 3 * CHUNK), lambda: (0, 0)),
                  pl.BlockSpec((CHUNK, 3 * CHUNK), lambda: (0, 0))],
        out_specs=pl.BlockSpec((N_ATT_HEADS, CHUNK, 3 * CHUNK), lambda: (0, 0, 0)),
        out_shape=jax.ShapeDtypeStruct((N_ATT_HEADS, CHUNK, 3 * CHUNK), F32),
        name="band_bias",
    )(rel_bias, buckets, inband)


ATT_COLS = 1024
MQK_COLS = 512
MV_COLS = 256
MO_COLS = 256
GG_COLS = 512
MG_COLS = 128
PROJ_PIECES = (ATT_COLS, MQK_COLS, MV_COLS, MO_COLS, GG_COLS, MG_COLS)
PROJ_DTYPES = (BF16, F32, BF16, F32, F32, F32)


def _rearrange_w_in(w_in):
    q = w_in[..., 0:512]
    k0, k1 = w_in[..., 512:576], w_in[..., 576:640]
    v0, v1 = w_in[..., 640:704], w_in[..., 704:768]
    rest = w_in[..., 768:1792]
    mg = w_in[..., 1792:1808]
    gg = w_in[..., 1808:2320]
    pad = jnp.zeros(w_in.shape[:-1] + (MG_COLS - mg.shape[-1],), w_in.dtype)
    return jnp.concatenate([q, k0, k0, k1, k1, v0, v0, v1, v1, rest, gg, mg, pad], axis=-1).astype(BF16)


def _proj_kernel(x_ref, mod_ref, w_ref, *rest):
    out_refs, h_scr = rest[:-1], rest[-1]
    h = _ln(x_ref[0]) * (1.0 + mod_ref[0, 1:2, :]) + mod_ref[0, 0:1, :]
    h_scr[...] = h.astype(BF16)
    off = 0
    for width, o_ref in zip(PROJ_PIECES, out_refs):
        o_ref[0] = jnp.dot(h_scr[...], w_ref[:, off:off + width],
                           preferred_element_type=F32).astype(o_ref.dtype)
        off += width


def _projection(x, mod6, w_in_r):
    b, s, d = x.shape
    tm = 512
    n = w_in_r.shape[-1]
    return pl.pallas_call(
        _proj_kernel,
        grid=(b, s // tm),
        in_specs=[pl.BlockSpec((1, tm, d), lambda i, j: (i, j, 0)),
                  pl.BlockSpec((1, 6, d), lambda i, j: (i, 0, 0)),
                  pl.BlockSpec((d, n), lambda i, j: (0, 0))],
        out_specs=[pl.BlockSpec((1, tm, w), lambda i, j: (i, j, 0)) for w in PROJ_PIECES],
        out_shape=[jax.ShapeDtypeStruct((b, s, w), dt) for w, dt in zip(PROJ_PIECES, PROJ_DTYPES)],
        scratch_shapes=[pltpu.VMEM((tm, d), BF16)],
        compiler_params=_cparams("parallel", "parallel"),
        name="proj",
    )(x, mod6, w_in_r)


def _attn_kernel(sink_ref, q_ref, kv_ref, bias_ref, o_ref, *, nb, tq):
    t = pl.program_id(1)
    lane = lax.broadcasted_iota(jnp.int32, (1, LANES), 1)
    lo = lane < HEAD_DIM
    col = lax.broadcasted_iota(jnp.int32, (1, 3 * CHUNK), 1)
    scale = HEAD_DIM ** -0.5
    for j in range(tq // CHUNK):
        n = t * (tq // CHUNK) + j
        ps = pl.multiple_of(jnp.maximum(n - 1, 0) * CHUNK, CHUNK)
        cs = pl.multiple_of(n * CHUNK, CHUNK)
        ns = pl.multiple_of(jnp.minimum(n + 1, nb - 1) * CHUNK, CHUNK)
        kvb = jnp.concatenate([kv_ref[0, pl.ds(ps, CHUNK), :],
                               kv_ref[0, pl.ds(cs, CHUNK), :],
                               kv_ref[0, pl.ds(ns, CHUNK), :]], axis=0)
        valid = jnp.logical_and(jnp.logical_or(col >= CHUNK, n > 0),
                                jnp.logical_or(col < 2 * CHUNK, n < nb - 1))
        q = q_ref[0, j * CHUNK:(j + 1) * CHUNK, :]
        outs = []
        for grp in range(N_ATT_HEADS // 2):
            kvh = grp // 2
            kd = kvb[:, kvh * LANES:(kvh + 1) * LANES]
            vd = kvb[:, (2 + kvh) * LANES:(3 + kvh) * LANES]
            qg = q[:, grp * LANES:(grp + 1) * LANES]
            zq = jnp.zeros_like(qg)
            q2 = jnp.concatenate([jnp.where(lo, qg, zq), jnp.where(lo, zq, qg)], axis=0)
            logits = lax.dot_general(q2, kd, NT_DIMS, preferred_element_type=F32) * scale
            acc = None
            for g in range(2):
                h = grp * 2 + g
                l = logits[g * CHUNK:(g + 1) * CHUNK] + bias_ref[h]
                l = jnp.where(valid, l, NEG_INF)
                sk = sink_ref[h]
                m = jnp.maximum(jnp.max(l, axis=1, keepdims=True), sk)
                p = jnp.exp(l - m)
                den = jnp.sum(p, axis=1, keepdims=True) + jnp.exp(sk - m)
                zv = jnp.zeros_like(vd)
                vh = jnp.where(lo, vd, zv) if g == 0 else jnp.where(lo, zv, vd)
                part = jnp.dot(p.astype(BF16), vh, preferred_element_type=F32) / den
                acc = part if acc is None else acc + part
            outs.append(acc)
        o_ref[0, j * CHUNK:(j + 1) * CHUNK, :] = jnp.concatenate(outs, axis=1).astype(o_ref.dtype)


def _attention(att, sink, bias):
    b, s, _ = att.shape
    tq = 512
    nb = s // CHUNK
    return pl.pallas_call(
        functools.partial(_attn_kernel, nb=nb, tq=tq),
        grid=(b, s // tq),
        in_specs=[pl.BlockSpec(memory_space=pltpu.SMEM),
                  pl.BlockSpec((1, tq, 512), lambda i, j: (i, j, 0)),
                  pl.BlockSpec((1, s, 512), lambda i, j: (i, 0, 1)),
                  pl.BlockSpec((N_ATT_HEADS, CHUNK, 3 * CHUNK), lambda i, j: (0, 0, 0))],
        out_specs=pl.BlockSpec((1, tq, 512), lambda i, j: (i, j, 0)),
        out_shape=jax.ShapeDtypeStruct((b, s, 512), BF16),
        compiler_params=_cparams("parallel", "parallel"),
        name="attn",
    )(sink, att, att, bias)


def _mlstm_kernel(mqk_ref, mv_ref, mo_ref, g4_ref, gb_ref, cw_ref, cb_ref, o_ref,
                  qk_scr, rrow_scr, rcol_scr, hf_scr, hb_scr, st_scr, m_scr, *, nc):
    L = CHUNK
    s_len = nc * L
    ng = 4 * N_ML_HEADS
    lane = lax.broadcasted_iota(jnp.int32, (1, LANES), 1)
    lo = lane < HEAD_DIM

    w0, w1, w2, cb = cw_ref[0:1, :], cw_ref[1:2, :], cw_ref[2:3, :], cb_ref[...]
    row = lax.broadcasted_iota(jnp.int32, (L, 1), 0)
    lane_qk = lax.broadcasted_iota(jnp.int32, (1, 4 * LANES), 1)
    kscale = jnp.where(lane_qk >= 2 * LANES, HEAD_DIM ** -0.5, 1.0).astype(F32)

    def conv_body(c, carry):
        r0 = pl.multiple_of(c * L, L)
        xc = mqk_ref[0, pl.ds(r0, L), :]
        pr = mqk_ref[0, pl.ds(jnp.maximum(r0 - 1, 0), 1), :]
        nx = mqk_ref[0, pl.ds(jnp.minimum(r0 + L, s_len - 1), 1), :]
        pr = jnp.where(c > 0, pr, 0.0)
        nx = jnp.where(c < nc - 1, nx, 0.0)
        xp = jnp.where(row == 0, pr, pltpu.roll(xc, 1, 0))
        xn = jnp.where(row == L - 1, nx, pltpu.roll(xc, L - 1, 0))
        y = w0 * xp + w1 * xc + w2 * xn + cb
        qk_scr[pl.ds(r0, L), :] = (_silu(y) * kscale).astype(BF16)
        return carry

    lax.fori_loop(0, nc, conv_body, 0)

    g = (g4_ref[0] + gb_ref[...]).reshape(nc * ng, L)
    kind = (lax.broadcasted_iota(jnp.int32, (nc * ng, 1), 0) % ng) // N_ML_HEADS
    logf = jnp.minimum(g, 0.0) - jnp.log1p(jnp.exp(-jnp.abs(g)))
    x = jnp.where(jnp.logical_or(kind == 1, kind == 3), logf, 0.0)
    iu = lax.broadcasted_iota(jnp.int32, (L, L), 0)
    it = lax.broadcasted_iota(jnp.int32, (L, L), 1)
    prefix = jnp.dot(x, (iu <= it).astype(F32), precision=HIGHEST, preferred_element_type=F32)
    suffix = jnp.dot(x, (iu >= it).astype(F32), precision=HIGHEST, preferred_element_type=F32)
    r = jnp.where(kind == 1, prefix, jnp.where(kind == 3, suffix, g))
    rrow_scr[...] = r.reshape(nc, ng, L)
    zpad = jnp.zeros((L - ng, L), F32)
    for c in range(nc):
        rcol_scr[c] = jnp.concatenate([r[c * ng:(c + 1) * ng], zpad], axis=0).T

    st_scr[...] = jnp.zeros(st_scr.shape, F32)
    m_scr[...] = jnp.zeros(m_scr.shape, F32)

    ones_even = jnp.where(lane == HEAD_DIM, 1.0, 0.0).astype(BF16)
    ones_odd = jnp.where(lane == 0, 1.0, 0.0).astype(BF16)

    def chunk_dir(c, dirn, h_scr):
        r0 = pl.multiple_of(c * L, L)
        qk = qk_scr[pl.ds(r0, L), :]
        vrows = mv_ref[0, pl.ds(r0, L), :]
        rr = rrow_scr[c]
        rc = rcol_scr[c]
        causal = (it <= iu) if dirn == 0 else (it >= iu)
        for grp in range(N_ML_HEADS // 2):
            qg = qk[:, grp * LANES:(grp + 1) * LANES]
            kg = qk[:, (2 + grp) * LANES:(3 + grp) * LANES]
            vg = vrows[:, grp * LANES:(grp + 1) * LANES]
            zb = jnp.zeros_like(qg)
            hv = []
            for odd in range(2):
                h = grp * 2 + odd
                hm = lo if odd == 0 else jnp.logical_not(lo)
                qh = jnp.where(hm, qg, zb)
                kh = jnp.where(hm, kg, zb)
                vext = jnp.where(hm, vg, ones_even if odd == 0 else ones_odd)
                ii, bi = dirn * 2 * N_ML_HEADS + h, dirn * 2 * N_ML_HEADS + N_ML_HEADS + h
                i_row, b_row = rr[ii:ii + 1, :], rr[bi:bi + 1, :]
                i_col, b_col = rc[:, ii:ii + 1], rc[:, bi:bi + 1]
                b_last = b_row[:, L - 1:L] if dirn == 0 else b_row[:, 0:1]
                sidx = dirn * N_ML_HEADS + h
                m_old = m_scr[sidx][:, 0:1]
                st = st_scr[sidx]
                dmat = jnp.where(causal, b_col - b_row + i_row, -jnp.inf)
                inter = b_col + m_old
                m_t = jnp.maximum(inter, jnp.max(dmat, axis=1, keepdims=True))
                smat = lax.dot_general(qh, kg, NT_DIMS, preferred_element_type=F32) * jnp.exp(dmat - m_t)
                iw = jnp.exp(inter - m_t)
                num = (jnp.dot(smat.astype(BF16), vext, preferred_element_type=F32)
                       + iw * jnp.dot(qh, st.astype(BF16), preferred_element_type=F32))
                den = num[:, HEAD_DIM:HEAD_DIM + 1] if odd == 0 else num[:, 0:1]
                hv.append(num / jnp.maximum(jnp.abs(den), jnp.exp(-m_t)))
                ws_col = b_last - b_col + i_col
                ws_row = b_last - b_row + i_row
                m_new = jnp.maximum(b_last + m_old, jnp.max(ws_row, axis=1, keepdims=True))
                wv = (jnp.exp(ws_col - m_new) * vext.astype(F32)).astype(BF16)
                decay = jnp.exp(b_last + m_old - m_new)
                st_scr[sidx] = decay * st + lax.dot_general(kh, wv, TN_DIMS, preferred_element_type=F32)
                m_scr[sidx] = jnp.broadcast_to(m_new, (1, LANES))
            h_scr[pl.ds(r0, L), grp * LANES:(grp + 1) * LANES] = jnp.where(lo, hv[0], hv[1])

    def body(i, carry):
        chunk_dir(i, 0, hf_scr)
        chunk_dir(nc - 1 - i, 1, hb_scr)
        return carry

    lax.fori_loop(0, nc, body, 0)
    o_ref[0] = (jax.nn.sigmoid(mo_ref[0]) * (hf_scr[...] + hb_scr[...])).astype(o_ref.dtype)


def _mlstm(mqk, mv, mo, mg, gate_b, conv_w, conv_b):
    b, s, _ = mqk.shape
    nc = s // CHUNK
    ng = 4 * N_ML_HEADS
    g4 = mg[:, :, :ng].reshape(b, nc, CHUNK, ng).transpose(0, 1, 3, 2)
    seq = lambda w: pl.BlockSpec((1, s, w), lambda i: (i, 0, 0))
    return pl.pallas_call(
        functools.partial(_mlstm_kernel, nc=nc),
        grid=(b,),
        in_specs=[seq(512), seq(256), seq(256),
                  pl.BlockSpec((1, nc, ng, CHUNK), lambda i: (i, 0, 0, 0)),
                  pl.BlockSpec((ng, 1), lambda i: (0, 0)),
                  pl.BlockSpec((ML_CONV, 512), lambda i: (0, 0)),
                  pl.BlockSpec((1, 512), lambda i: (0, 0))],
        out_specs=seq(256),
        out_shape=jax.ShapeDtypeStruct((b, s, 256), BF16),
        scratch_shapes=[pltpu.VMEM((s, 512), BF16),
                        pltpu.VMEM((nc, ng, CHUNK), F32),
                        pltpu.VMEM((nc, CHUNK, LANES), F32),
                        pltpu.VMEM((s, 256), F32),
                        pltpu.VMEM((s, 256), F32),
                        pltpu.VMEM((2 * N_ML_HEADS, LANES, LANES), F32),
                        pltpu.VMEM((2 * N_ML_HEADS, 1, LANES), F32)],
        compiler_params=_cparams("parallel"),
        name="mlstm",
    )(mqk, mv, mo, g4, gate_b.reshape(ng, 1), conv_w, conv_b.reshape(1, 512))


def _gmlp_kernel(gg_ref, ws_ref, bs_ref, o_ref, *, tg):
    lane = lax.broadcasted_iota(jnp.int32, (1, 2 * LANES), 1)
    for j in range(tg // CHUNK):
        rows = slice(j * CHUNK, (j + 1) * CHUNK)
        u = _gelu(gg_ref[0, rows, 0:256])
        v = _ln(_gelu(gg_ref[0, rows, 256:512])).astype(BF16)
        zv = jnp.zeros_like(v)
        s = None
        for g in range(N_GM_GROUPS):
            ing = jnp.logical_and(lane >= g * HEAD_DIM, lane < (g + 1) * HEAD_DIM)
            part = jnp.dot(ws_ref[g], jnp.where(ing, v, zv), preferred_element_type=F32)
            part = part + jnp.where(ing, bs_ref[:, g:g + 1], 0.0)
            s = part if s is None else s + part
        o_ref[0, rows, :] = (u * s).astype(o_ref.dtype)


def _gmlp(gg, w_s, b_s):
    b, s, _ = gg.shape
    tg = 512
    return pl.pallas_call(
        functools.partial(_gmlp_kernel, tg=tg),
        grid=(b, s // tg),
        in_specs=[pl.BlockSpec((1, tg, 512), lambda i, j: (i, j, 0)),
                  pl.BlockSpec((N_GM_GROUPS, CHUNK, CHUNK), lambda i, j: (0, 0, 0)),
                  pl.BlockSpec((CHUNK, N_GM_GROUPS), lambda i, j: (0, 0))],
        out_specs=pl.BlockSpec((1, tg, 256), lambda i, j: (i, j, 0)),
        out_shape=jax.ShapeDtypeStruct((b, s, 256), BF16),
        compiler_params=_cparams("parallel", "parallel"),
        name="gmlp",
    )(gg, w_s.astype(BF16), b_s.T)


def _mixout_kernel(att_ref, ml_ref, gm_ref, x_ref, mod_ref, wo_ref, lng_ref, lnb_ref, wr_ref,
                   x1_ref, h2_ref, lg_ref, *, alpha):
    mix = jnp.dot(att_ref[0], wo_ref[0:512, :], preferred_element_type=F32)
    mix = mix + jnp.dot(ml_ref[0], wo_ref[512:768, :], preferred_element_type=F32)
    mix = mix + jnp.dot(gm_ref[0], wo_ref[768:1024, :], preferred_element_type=F32)
    y = alpha * x_ref[0] + (1.0 + mod_ref[0, 2:3, :]) * mix
    x1 = _ln(y) * lng_ref[...] + lnb_ref[...]
    x1_ref[0] = x1
    h2 = _ln(x1) * (1.0 + mod_ref[0, 4:5, :]) + mod_ref[0, 3:4, :]
    h2_ref[0] = h2.astype(h2_ref.dtype)
    lg_ref[0] = jnp.dot(h2, wr_ref[...], precision=HIGHEST, preferred_element_type=F32)


def _mixout(att, ml, gm, x, mod6, w_out, ln_g, ln_b, w_router_p, alpha):
    b, s, d = x.shape
    tm = 512
    row = lambda w: pl.BlockSpec((1, tm, w), lambda i, j: (i, j, 0))
    full = lambda a: pl.BlockSpec(a.shape, lambda i, j: (0,) * a.ndim)
    return pl.pallas_call(
        functools.partial(_mixout_kernel, alpha=alpha),
        grid=(b, s // tm),
        in_specs=[row(512), row(256), row(256), row(d),
                  pl.BlockSpec((1, 6, d), lambda i, j: (i, 0, 0)),
                  full(w_out), full(ln_g), full(ln_b), full(w_router_p)],
        out_specs=[row(d), row(d), row(LANES)],
        out_shape=[jax.ShapeDtypeStruct((b, s, d), F32),
                   jax.ShapeDtypeStruct((b, s, d), BF16),
                   jax.ShapeDtypeStruct((b, s, LANES), F32)],
        compiler_params=_cparams("parallel", "parallel"),
        name="mixout",
    )(att, ml, gm, x, mod6, w_out, ln_g, ln_b, w_router_p)


def _route_kernel(lg_ref, pos_ref, aff_ref, *, cap, ne, nc):
    L = CHUNK
    lane = lax.broadcasted_iota(jnp.int32, (1, LANES), 1)
    l = jnp.where(lane < ne, lg_ref[0], -jnp.inf)
    e = jnp.exp(l - jnp.max(l, axis=1, keepdims=True))
    aff = e / jnp.sum(e, axis=1, keepdims=True)
    aff_ref[0] = aff
    bits = pltpu.bitcast(aff, jnp.int32)

    def bit_body(k, thr):
        cand = jnp.bitwise_or(thr, jnp.left_shift(jnp.int32(1), 30 - k))
        cnt = jnp.sum((bits >= cand).astype(jnp.int32), axis=0, keepdims=True)
        return jnp.where(cnt >= cap, cand, thr)

    thr = lax.fori_loop(0, 31, bit_body, jnp.zeros((1, LANES), jnp.int32))
    n_gt = jnp.sum((bits > thr).astype(jnp.int32), axis=0, keepdims=True)
    need = (cap - n_gt).astype(F32)

    it = lax.broadcasted_iota(jnp.int32, (L, L), 0)
    iu = lax.broadcasted_iota(jnp.int32, (L, L), 1)
    before = (iu < it).astype(BF16)

    def chunk_body(c, carry):
        run_eq, run_sel = carry
        r0 = pl.multiple_of(c * L, L)
        bc = pltpu.bitcast(aff_ref[0, pl.ds(r0, L), :], jnp.int32)
        eq = bc == thr
        eqf = jnp.where(eq, 1.0, 0.0)
        eq_before = jnp.dot(before, eqf.astype(BF16), preferred_element_type=F32) + run_eq
        sel = jnp.logical_or(bc > thr, jnp.logical_and(eq, eq_before < need))
        self_ = jnp.where(sel, 1.0, 0.0)
        sel_before = jnp.dot(before, self_.astype(BF16), preferred_element_type=F32) + run_sel
        pos_ref[0, pl.ds(r0, L), :] = jnp.where(sel, sel_before.astype(jnp.int32), -1)
        return (run_eq + jnp.sum(eqf, axis=0, keepdims=True),
                run_sel + jnp.sum(self_, axis=0, keepdims=True))

    zero = jnp.zeros((1, LANES), F32)
    lax.fori_loop(0, nc, chunk_body, (zero, zero))


def _route(logits, ne, cap):
    b, s, _ = logits.shape
    blk = pl.BlockSpec((1, s, LANES), lambda i: (i, 0, 0))
    return pl.pallas_call(
        functools.partial(_route_kernel, cap=cap, ne=ne, nc=s // CHUNK),
        grid=(b,),
        in_specs=[blk],
        out_specs=[blk, blk],
        out_shape=[jax.ShapeDtypeStruct((b, s, LANES), jnp.int32),
                   jax.ShapeDtypeStruct((b, s, LANES), F32)],
        compiler_params=_cparams("parallel"),
        name="route",
    )(logits)


def _ffn_kernel(pos_ref, h2_ref, wg_ref, wu_ref, wd_ref, o_ref, acc_scr, *, cap, tf):
    s = h2_ref.shape[1]
    slot = lax.broadcasted_iota(jnp.int32, (cap, s), 0)
    onehot = jnp.where(pos_ref[0, 0] == slot, 1.0, 0.0).astype(BF16)
    xg = jnp.dot(onehot, h2_ref[0], preferred_element_type=F32).astype(BF16)
    f = wg_ref.shape[2]
    for k in range(f // tf):
        cols = slice(k * tf, (k + 1) * tf)
        a = jnp.dot(xg, wg_ref[0, :, cols], preferred_element_type=F32)
        u = jnp.dot(xg, wu_ref[0, :, cols], preferred_element_type=F32)
        hid = (_silu(a) * u).astype(BF16)
        part = jnp.dot(hid, wd_ref[0, cols, :], preferred_element_type=F32)
        if k == 0:
            acc_scr[...] = part
        else:
            acc_scr[...] += part
    o_ref[0, 0] = acc_scr[...].astype(o_ref.dtype)


def _expert_ffn(pos_t, h2, w_gate, w_up, w_down, cap):
    b, s, d = h2.shape
    ne, _, f = w_gate.shape
    return pl.pallas_call(
        functools.partial(_ffn_kernel, cap=cap, tf=512),
        grid=(ne, b),
        in_specs=[pl.BlockSpec((1, 1, 1, s), lambda e, i: (i, e, 0, 0)),
                  pl.BlockSpec((1, s, d), lambda e, i: (i, 0, 0)),
                  pl.BlockSpec((1, d, f), lambda e, i: (e, 0, 0)),
                  pl.BlockSpec((1, d, f), lambda e, i: (e, 0, 0)),
                  pl.BlockSpec((1, f, d), lambda e, i: (e, 0, 0))],
        out_specs=pl.BlockSpec((1, 1, cap, d), lambda e, i: (i, e, 0, 0)),
        out_shape=jax.ShapeDtypeStruct((b, ne, cap, d), BF16),
        scratch_shapes=[pltpu.VMEM((cap, d), F32)],
        compiler_params=_cparams("parallel", "parallel"),
        name="expert_ffn",
    )(pos_t, h2, w_gate, w_up, w_down)


def _scatter_kernel(pos_ref, aff_ref, ye_ref, x1_ref, mod_ref, lng_ref, lnb_ref, o_ref, acc_scr,
                    *, cap, ne, alpha):
    e = pl.program_id(2)
    ts = pos_ref.shape[1]
    lane = lax.broadcasted_iota(jnp.int32, (1, LANES), 1)
    mine = lane == e
    pos = jnp.sum(jnp.where(mine, pos_ref[0], 0), axis=1, keepdims=True)
    gate = jnp.sum(jnp.where(mine, aff_ref[0], 0.0), axis=1, keepdims=True)
    slot = lax.broadcasted_iota(jnp.int32, (ts, cap), 1)
    onehot = jnp.where(pos == slot, 1.0, 0.0).astype(BF16)
    contrib = gate * jnp.dot(onehot, ye_ref[0, 0], preferred_element_type=F32)

    @pl.when(e == 0)
    def _():
        acc_scr[...] = contrib

    @pl.when(e > 0)
    def _():
        acc_scr[...] += contrib

    @pl.when(e == ne - 1)
    def _():
        y = alpha * x1_ref[0] + (1.0 + mod_ref[0, 5:6, :]) * acc_scr[...]
        o_ref[0] = _ln(y) * lng_ref[...] + lnb_ref[...]


def _scatter(pos, aff, ye, x1, mod6, ln_g, ln_b, alpha):
    b, s, d = x1.shape
    ne, cap = ye.shape[1], ye.shape[2]
    ts = 512
    return pl.pallas_call(
        functools.partial(_scatter_kernel, cap=cap, ne=ne, alpha=alpha),
        grid=(b, s // ts, ne),
        in_specs=[pl.BlockSpec((1, ts, LANES), lambda i, j, e: (i, j, 0)),
                  pl.BlockSpec((1, ts, LANES), lambda i, j, e: (i, j, 0)),
                  pl.BlockSpec((1, 1, cap, d), lambda i, j, e: (i, e, 0, 0)),
                  pl.BlockSpec((1, ts, d), lambda i, j, e: (i, j, 0)),
                  pl.BlockSpec((1, 6, d), lambda i, j, e: (i, 0, 0)),
                  pl.BlockSpec((1, d), lambda i, j, e: (0, 0)),
                  pl.BlockSpec((1, d), lambda i, j, e: (0, 0))],
        out_specs=pl.BlockSpec((1, ts, d), lambda i, j, e: (i, j, 0)),
        out_shape=jax.ShapeDtypeStruct((b, s, d), F32),
        scratch_shapes=[pltpu.VMEM((ts, d), F32)],
        compiler_params=_cparams("parallel", "parallel", "arbitrary"),
        name="scatter_ln",
    )(pos, aff, ye, x1, mod6, ln_g, ln_b)


def kernel(x, c, w_ada, b_ada, w_in, conv_w, conv_b, gate_b, sink, rel_bias, w_s, b_s, w_out, w_router,
           w_gate, w_up, w_down, ln_g, ln_b):
    depth = w_ada.shape[0]
    b, s, d = x.shape
    ne = w_router.shape[-1]
    cap = max(1, min(s, EC_CAPACITY_FACTOR * s // ne))
    alpha = float((2 * depth) ** 0.25)

    mod = _modulation(c, w_ada, b_ada).reshape(depth, b, 6, d)
    bias = _band_bias(rel_bias)
    w_in_r = _rearrange_w_in(w_in)
    w_router_p = jnp.pad(w_router, ((0, 0), (0, 0), (0, LANES - ne)))

    for l in range(depth):
        att, mqk, mv, mo, gg, mg = _projection(x, mod[l], w_in_r[l])
        att_o = _attention(att, sink[l], bias)
        ml_o = _mlstm(mqk, mv, mo, mg, gate_b[l], conv_w[l], conv_b[l])
        gm_o = _gmlp(gg, w_s[l], b_s[l])
        x1, h2, logits = _mixout(att_o, ml_o, gm_o, x, mod[l], w_out[l].astype(BF16),
                                 ln_g[l, 0:1], ln_b[l, 0:1], w_router_p[l], alpha)
        pos, aff = _route(logits, ne, cap)
        pos_t = pos[:, :, :ne].transpose(0, 2, 1).reshape(b, ne, 1, s)
        ye = _expert_ffn(pos_t, h2, w_gate[l].astype(BF16), w_up[l].astype(BF16),
                         w_down[l].astype(BF16), cap)
        x = _scatter(pos, aff, ye, x1, mod[l], ln_g[l, 1:2], ln_b[l, 1:2], alpha)
    return x
```

```python
import functools

import numpy as np
import jax
import jax.numpy as jnp
from jax import lax
from jax.experimental import pallas as pl
from jax.experimental.pallas import tpu as pltpu

F32 = jnp.float32
BF16 = jnp.bfloat16
HIGHEST = lax.Precision.HIGHEST

HEAD_DIM = 64
LANES = 128
CHUNK = 128
N_ATT_HEADS = 8
N_KV_HEADS = 2
N_ML_HEADS = 4
N_GM_GROUPS = 4
ML_CONV = 3
N_REL_BUCKETS = 32
REL_MAX_DIST = 128
WINDOW = 128
EC_CAPACITY_FACTOR = 2
LN_EPS = 1e-5
NEG_INF = -1e30
VMEM_LIMIT = 56 * 1024 * 1024

NT_DIMS = (((1,), (1,)), ((), ()))
TN_DIMS = (((0,), (0,)), ((), ()))


def _cparams(*sem):
    return pltpu.CompilerParams(dimension_semantics=sem, vmem_limit_bytes=VMEM_LIMIT)


def _ln(x):
    mu = jnp.mean(x, axis=-1, keepdims=True)
    xc = x - mu
    var = jnp.mean(xc * xc, axis=-1, keepdims=True)
    return xc * lax.rsqrt(var + LN_EPS)


def _gelu(x):
    return 0.5 * x * (1.0 + lax.erf(x * (2.0 ** -0.5)))


def _silu(x):
    return x * jax.nn.sigmoid(x)


def _mod_kernel(c_ref, w_ref, b_ref, o_ref):
    sc = _silu(c_ref[...])
    o_ref[0] = jnp.dot(sc, w_ref[0], precision=HIGHEST, preferred_element_type=F32) + b_ref[0]


def _modulation(c, w_ada, b_ada):
    depth, d, n = w_ada.shape
    b = c.shape[0]
    tn = 1536
    return pl.pallas_call(
        _mod_kernel,
        grid=(depth, n // tn),
        in_specs=[pl.BlockSpec((b, d), lambda l, j: (0, 0)),
                  pl.BlockSpec((1, d, tn), lambda l, j: (l, 0, j)),
                  pl.BlockSpec((1, 1, tn), lambda l, j: (l, 0, j))],
        out_specs=pl.BlockSpec((1, b, tn), lambda l, j: (l, 0, j)),
        out_shape=jax.ShapeDtypeStruct((depth, b, n), F32),
        compiler_params=_cparams("parallel", "parallel"),
        name="modulation",
    )(c, w_ada, b_ada.reshape(depth, 1, n))


def _t5_buckets(rel):
    nb = N_REL_BUCKETS // 2
    max_exact = nb // 2
    ret = np.where(rel > 0, nb, 0)
    n = np.abs(rel)
    large = max_exact + (np.log(np.maximum(n, 1) / max_exact) / np.log(REL_MAX_DIST / max_exact)
                         * (nb - max_exact)).astype(np.int32)
    large = np.minimum(large, nb - 1)
    return (ret + np.where(n < max_exact, n, large)).astype(np.int32)


def _bias_kernel(rb_ref, bk_ref, mk_ref, o_ref):
    bk = bk_ref[...]
    inband = mk_ref[...] != 0
    for h in range(N_ATT_HEADS):
        acc = jnp.zeros(bk.shape, F32)
        for k in range(N_REL_BUCKETS):
            acc = jnp.where(bk == k, rb_ref[k, h], acc)
        o_ref[h] = jnp.where(inband, acc, NEG_INF)


def _band_bias(rel_bias):
    qi = np.arange(CHUNK)[:, None]
    sj = np.arange(3 * CHUNK)[None, :]
    rel = sj - CHUNK - qi
    buckets = jnp.asarray(_t5_buckets(rel), jnp.int32)
    inband = jnp.asarray((np.abs(rel) <= WINDOW).astype(np.int32))
    return pl.pallas_call(
        _bias_kernel,
        in_specs=[pl.BlockSpec(memory_space=pltpu.SMEM),
                  pl.BlockSpec((CHUNK, 3 * CHUNK), lambda: (0, 0)),
                  pl.BlockSpec((CHUNK, 3 * CHUNK), lambda: (0, 0))],
        out_specs=pl.BlockSpec((N_ATT_HEADS, CHUNK, 3 * CHUNK), lambda: (0, 0, 0)),
        out_shape=jax.ShapeDtypeStruct((N_ATT_HEADS, CHUNK, 3 * CHUNK), F32),
        name="band_bias",
    )(rel_bias, buckets, inband)


ATT_COLS = 1024
MQK_COLS = 512
MV_COLS = 256
MO_COLS = 256
GG_COLS = 512
MG_COLS = 128
PROJ_PIECES = (ATT_COLS, MQK_COLS, MV_COLS, MO_COLS, GG_COLS, MG_COLS)
PROJ_DTYPES = (BF16, F32, BF16, F32, F32, F32)


def _rearrange_w_in(w_in):
    q = w_in[..., 0:512]
    k0, k1 = w_in[..., 512:576], w_in[..., 576:640]
    v0, v1 = w_in[..., 640:704], w_in[..., 704:768]
    rest = w_in[..., 768:1792]
    mg = w_in[..., 1792:1808]
    gg = w_in[..., 1808:2320]
    pad = jnp.zeros(w_in.shape[:-1] + (MG_COLS - mg.shape[-1],), w_in.dtype)
    return jnp.concatenate([q, k0, k0, k1, k1, v0, v0, v1, v1, rest, gg, mg, pad], axis=-1).astype(BF16)


def _proj_kernel(x_ref, mod_ref, w_ref, *rest):
    out_refs, h_scr = rest[:-1], rest[-1]
    h = _ln(x_ref[0]) * (1.0 + mod_ref[0, 0, 1:2, :]) + mod_ref[0, 0, 0:1, :]
    h_scr[...] = h.astype(BF16)
    off = 0
    for width, o_ref in zip(PROJ_PIECES, out_refs):
        o_ref[0] = jnp.dot(h_scr[...], w_ref[0, :, off:off + width],
                           preferred_element_type=F32).astype(o_ref.dtype)
        off += width


def _projection(x, mod, w_in_r, l):
    b, s, d = x.shape
    tm = 512
    n = w_in_r.shape[-1]
    return pl.pallas_call(
        _proj_kernel,
        grid=(b, s // tm),
        in_specs=[pl.BlockSpec((1, tm, d), lambda i, j: (i, j, 0)),
                  pl.BlockSpec((1, 1, 6, d), lambda i, j: (l, i, 0, 0)),
                  pl.BlockSpec((1, d, n), lambda i, j: (l, 0, 0))],
        out_specs=[pl.BlockSpec((1, tm, w), lambda i, j: (i, j, 0)) for w in PROJ_PIECES],
        out_shape=[jax.ShapeDtypeStruct((b, s, w), dt) for w, dt in zip(PROJ_PIECES, PROJ_DTYPES)],
        scratch_shapes=[pltpu.VMEM((tm, d), BF16)],
        compiler_params=_cparams("parallel", "parallel"),
        name="proj",
    )(x, mod, w_in_r)


def _attn_kernel(sink_ref, q_ref, kv_ref, bias_ref, o_ref, *, nb, tq):
    t = pl.program_id(1)
    lane = lax.broadcasted_iota(jnp.int32, (1, LANES), 1)
    lo = lane < HEAD_DIM
    col = lax.broadcasted_iota(jnp.int32, (1, 3 * CHUNK), 1)
    scale = HEAD_DIM ** -0.5
    for j in range(tq // CHUNK):
        n = t * (tq // CHUNK) + j
        ps = pl.multiple_of(jnp.maximum(n - 1, 0) * CHUNK, CHUNK)
        cs = pl.multiple_of(n * CHUNK, CHUNK)
        ns = pl.multiple_of(jnp.minimum(n + 1, nb - 1) * CHUNK, CHUNK)
        kvb = jnp.concatenate([kv_ref[0, pl.ds(ps, CHUNK), :],
                               kv_ref[0, pl.ds(cs, CHUNK), :],
                               kv_ref[0, pl.ds(ns, CHUNK), :]], axis=0)
        valid = jnp.logical_and(jnp.logical_or(col >= CHUNK, n > 0),
                                jnp.logical_or(col < 2 * CHUNK, n < nb - 1))
        q = q_ref[0, j * CHUNK:(j + 1) * CHUNK, :]
        outs = []
        for grp in range(N_ATT_HEADS // 2):
            kvh = grp // 2
            kd = kvb[:, kvh * LANES:(kvh + 1) * LANES]
            vd = kvb[:, (2 + kvh) * LANES:(3 + kvh) * LANES]
            qg = q[:, grp * LANES:(grp + 1) * LANES]
            zq = jnp.zeros_like(qg)
            q2 = jnp.concatenate([jnp.where(lo, qg, zq), jnp.where(lo, zq, qg)], axis=0)
            logits = lax.dot_general(q2, kd, NT_DIMS, preferred_element_type=F32) * scale
            acc = None
            for g in range(2):
                h = grp * 2 + g
                l = logits[g * CHUNK:(g + 1) * CHUNK] + bias_ref[h]
                l = jnp.where(valid, l, NEG_INF)
                sk = sink_ref[h]
                m = jnp.maximum(jnp.max(l, axis=1, keepdims=True), sk)
                p = jnp.exp(l - m)
                den = jnp.sum(p, axis=1, keepdims=True) + jnp.exp(sk - m)
                zv = jnp.zeros_like(vd)
                vh = jnp.where(lo, vd, zv) if g == 0 else jnp.where(lo, zv, vd)
                part = jnp.dot(p.astype(BF16), vh, preferred_element_type=F32) / den
                acc = part if acc is None else acc + part
            outs.append(acc)
        o_ref[0, j * CHUNK:(j + 1) * CHUNK, :] = jnp.concatenate(outs, axis=1).astype(o_ref.dtype)


def _attention(att, sink, bias):
    b, s, _ = att.shape
    tq = 512
    nb = s // CHUNK
    return pl.pallas_call(
        functools.partial(_attn_kernel, nb=nb, tq=tq),
        grid=(b, s // tq),
        in_specs=[pl.BlockSpec(memory_space=pltpu.SMEM),
                  pl.BlockSpec((1, tq, 512), lambda i, j: (i, j, 0)),
                  pl.BlockSpec((1, s, 512), lambda i, j: (i, 0, 1)),
                  pl.BlockSpec((N_ATT_HEADS, CHUNK, 3 * CHUNK), lambda i, j: (0, 0, 0))],
        out_specs=pl.BlockSpec((1, tq, 512), lambda i, j: (i, j, 0)),
        out_shape=jax.ShapeDtypeStruct((b, s, 512), BF16),
        compiler_params=_cparams("parallel", "parallel"),
        name="attn",
    )(sink, att, att, bias)


def _mlstm_kernel(mqk_ref, mv_ref, mo_ref, g4_ref, gb_ref, cw_ref, cb_ref, o_ref,
                  qk_scr, rrow_scr, rcol_scr, hf_scr, hb_scr, st_scr, m_scr, *, nc):
    L = CHUNK
    s_len = nc * L
    ng = 4 * N_ML_HEADS
    lane = lax.broadcasted_iota(jnp.int32, (1, LANES), 1)
    lo = lane < HEAD_DIM

    w0, w1, w2, cb = cw_ref[0:1, :], cw_ref[1:2, :], cw_ref[2:3, :], cb_ref[...]
    row = lax.broadcasted_iota(jnp.int32, (L, 1), 0)
    lane_qk = lax.broadcasted_iota(jnp.int32, (1, 4 * LANES), 1)
    kscale = jnp.where(lane_qk >= 2 * LANES, HEAD_DIM ** -0.5, 1.0).astype(F32)

    def conv_body(c, carry):
        r0 = pl.multiple_of(c * L, L)
        xc = mqk_ref[0, pl.ds(r0, L), :]
        pr = mqk_ref[0, pl.ds(jnp.maximum(r0 - 1, 0), 1), :]
        nx = mqk_ref[0, pl.ds(jnp.minimum(r0 + L, s_len - 1), 1), :]
        pr = jnp.where(c > 0, pr, 0.0)
        nx = jnp.where(c < nc - 1, nx, 0.0)
        xp = jnp.where(row == 0, pr, pltpu.roll(xc, 1, 0))
        xn = jnp.where(row == L - 1, nx, pltpu.roll(xc, L - 1, 0))
        y = w0 * xp + w1 * xc + w2 * xn + cb
        qk_scr[pl.ds(r0, L), :] = (_silu(y) * kscale).astype(BF16)
        return carry

    lax.fori_loop(0, nc, conv_body, 0)

    g = (g4_ref[0] + gb_ref[...]).reshape(nc * ng, L)
    kind = (lax.broadcasted_iota(jnp.int32, (nc * ng, 1), 0) % ng) // N_ML_HEADS
    logf = jnp.minimum(g, 0.0) - jnp.log1p(jnp.exp(-jnp.abs(g)))
    x = jnp.where(jnp.logical_or(kind == 1, kind == 3), logf, 0.0)
    iu = lax.broadcasted_iota(jnp.int32, (L, L), 0)
    it = lax.broadcasted_iota(jnp.int32, (L, L), 1)
    prefix = jnp.dot(x, (iu <= it).astype(F32), precision=HIGHEST, preferred_element_type=F32)
    suffix = jnp.dot(x, (iu >= it).astype(F32), precision=HIGHEST, preferred_element_type=F32)
    r = jnp.where(kind == 1, prefix, jnp.where(kind == 3, suffix, g))
    rrow_scr[...] = r.reshape(nc, ng, L)
    zpad = jnp.zeros((L - ng, L), F32)
    for c in range(nc):
        rcol_scr[c] = jnp.concatenate([r[c * ng:(c + 1) * ng], zpad], axis=0).T

    st_scr[...] = jnp.zeros(st_scr.shape, F32)
    m_scr[...] = jnp.zeros(m_scr.shape, F32)

    ones_even = jnp.where(lane == HEAD_DIM, 1.0, 0.0).astype(BF16)
    ones_odd = jnp.where(lane == 0, 1.0, 0.0).astype(BF16)

    def chunk_dir(c, dirn, h_scr):
        r0 = pl.multiple_of(c * L, L)
        qk = qk_scr[pl.ds(r0, L), :]
        vrows = mv_ref[0, pl.ds(r0, L), :]
        rr = rrow_scr[c]
        rc = rcol_scr[c]
        causal = (it <= iu) if dirn == 0 else (it >= iu)
        for grp in range(N_ML_HEADS // 2):
            qg = qk[:, grp * LANES:(grp + 1) * LANES]
            kg = qk[:, (2 + grp) * LANES:(3 + grp) * LANES]
            vg = vrows[:, grp * LANES:(grp + 1) * LANES]
            zb = jnp.zeros_like(qg)
            hv = []
            for odd in range(2):
                h = grp * 2 + odd
                hm = lo if odd == 0 else jnp.logical_not(lo)
                qh = jnp.where(hm, qg, zb)
                kh = jnp.where(hm, kg, zb)
                vext = jnp.where(hm, vg, ones_even if odd == 0 else ones_odd)
                ii, bi = dirn * 2 * N_ML_HEADS + h, dirn * 2 * N_ML_HEADS + N_ML_HEADS + h
                i_row, b_row = rr[ii:ii + 1, :], rr[bi:bi + 1, :]
                i_col, b_col = rc[:, ii:ii + 1], rc[:, bi:bi + 1]
                b_last = b_row[:, L - 1:L] if dirn == 0 else b_row[:, 0:1]
                sidx = dirn * N_ML_HEADS + h
                m_old = m_scr[sidx][:, 0:1]
                st = st_scr[sidx]
                dmat = jnp.where(causal, b_col - b_row + i_row, -jnp.inf)
                inter = b_col + m_old
                m_t = jnp.maximum(inter, jnp.max(dmat, axis=1, keepdims=True))
                smat = lax.dot_general(qh, kg, NT_DIMS, preferred_element_type=F32) * jnp.exp(dmat - m_t)
                iw = jnp.exp(inter - m_t)
                num = (jnp.dot(smat.astype(BF16), vext, preferred_element_type=F32)
                       + iw * jnp.dot(qh, st.astype(BF16), preferred_element_type=F32))
                den = num[:, HEAD_DIM:HEAD_DIM + 1] if odd == 0 else num[:, 0:1]
                hv.append(num / jnp.maximum(jnp.abs(den), jnp.exp(-m_t)))
                ws_col = b_last - b_col + i_col
                ws_row = b_last - b_row + i_row
                m_new = jnp.maximum(b_last + m_old, jnp.max(ws_row, axis=1, keepdims=True))
                wv = (jnp.exp(ws_col - m_new) * vext.astype(F32)).astype(BF16)
                decay = jnp.exp(b_last + m_old - m_new)
                st_scr[sidx] = decay * st + lax.dot_general(kh, wv, TN_DIMS, preferred_element_type=F32)
                m_scr[sidx] = jnp.broadcast_to(m_new, (1, LANES))
            h_scr[pl.ds(r0, L), grp * LANES:(grp + 1) * LANES] = jnp.where(lo, hv[0], hv[1])

    def body(i, carry):
        chunk_dir(i, 0, hf_scr)
        chunk_dir(nc - 1 - i, 1, hb_scr)
        return carry

    lax.fori_loop(0, nc, body, 0)
    o_ref[0] = (jax.nn.sigmoid(mo_ref[0]) * (hf_scr[...] + hb_scr[...])).astype(o_ref.dtype)


def _mlstm(mqk, mv, mo, mg, gate_b, conv_w, conv_b):
    b, s, _ = mqk.shape
    nc = s // CHUNK
    ng = 4 * N_ML_HEADS
    g4 = mg[:, :, :ng].reshape(b, nc, CHUNK, ng).transpose(0, 1, 3, 2)
    seq = lambda w: pl.BlockSpec((1, s, w), lambda i: (i, 0, 0))
    return pl.pallas_call(
        functools.partial(_mlstm_kernel, nc=nc),
        grid=(b,),
        in_specs=[seq(512), seq(256), seq(256),
                  pl.BlockSpec((1, nc, ng, CHUNK), lambda i: (i, 0, 0, 0)),
                  pl.BlockSpec((ng, 1), lambda i: (0, 0)),
                  pl.BlockSpec((ML_CONV, 512), lambda i: (0, 0)),
                  pl.BlockSpec((1, 512), lambda i: (0, 0))],
        out_specs=seq(256),
        out_shape=jax.ShapeDtypeStruct((b, s, 256), BF16),
        scratch_shapes=[pltpu.VMEM((s, 512), BF16),
                        pltpu.VMEM((nc, ng, CHUNK), F32),
                        pltpu.VMEM((nc, CHUNK, LANES), F32),
                        pltpu.VMEM((s, 256), F32),
                        pltpu.VMEM((s, 256), F32),
                        pltpu.VMEM((2 * N_ML_HEADS, LANES, LANES), F32),
                        pltpu.VMEM((2 * N_ML_HEADS, 1, LANES), F32)],
        compiler_params=_cparams("parallel"),
        name="mlstm",
    )(mqk, mv, mo, g4, gate_b.reshape(ng, 1), conv_w, conv_b.reshape(1, 512))


def _gmlp_kernel(gg_ref, ws_ref, bs_ref, o_ref, *, tg):
    lane = lax.broadcasted_iota(jnp.int32, (1, 2 * LANES), 1)
    for j in range(tg // CHUNK):
        rows = slice(j * CHUNK, (j + 1) * CHUNK)
        u = _gelu(gg_ref[0, rows, 0:256])
        v = _ln(_gelu(gg_ref[0, rows, 256:512])).astype(BF16)
        zv = jnp.zeros_like(v)
        s = None
        for g in range(N_GM_GROUPS):
            ing = jnp.logical_and(lane >= g * HEAD_DIM, lane < (g + 1) * HEAD_DIM)
            part = jnp.dot(ws_ref[g], jnp.where(ing, v, zv), preferred_element_type=F32)
            part = part + jnp.where(ing, bs_ref[:, g:g + 1], 0.0)
            s = part if s is None else s + part
        o_ref[0, rows, :] = (u * s).astype(o_ref.dtype)


def _gmlp(gg, w_s, b_s):
    b, s, _ = gg.shape
    tg = 512
    return pl.pallas_call(
        functools.partial(_gmlp_kernel, tg=tg),
        grid=(b, s // tg),
        in_specs=[pl.BlockSpec((1, tg, 512), lambda i, j: (i, j, 0)),
                  pl.BlockSpec((N_GM_GROUPS, CHUNK, CHUNK), lambda i, j: (0, 0, 0)),
                  pl.BlockSpec((CHUNK, N_GM_GROUPS), lambda i, j: (0, 0))],
        out_specs=pl.BlockSpec((1, tg, 256), lambda i, j: (i, j, 0)),
        out_shape=jax.ShapeDtypeStruct((b, s, 256), BF16),
        compiler_params=_cparams("parallel", "parallel"),
        name="gmlp",
    )(gg, w_s.astype(BF16), b_s.T)


def _mixout_kernel(att_ref, ml_ref, gm_ref, x_ref, mod_ref, wo_ref, lng_ref, lnb_ref, wr_ref,
                   x1_ref, h2_ref, lg_ref, *, alpha):
    mix = jnp.dot(att_ref[0], wo_ref[0, 0:512, :], preferred_element_type=F32)
    mix = mix + jnp.dot(ml_ref[0], wo_ref[0, 512:768, :], preferred_element_type=F32)
    mix = mix + jnp.dot(gm_ref[0], wo_ref[0, 768:1024, :], preferred_element_type=F32)
    y = alpha * x_ref[0] + (1.0 + mod_ref[0, 0, 2:3, :]) * mix
    x1 = _ln(y) * lng_ref[...] + lnb_ref[...]
    x1_ref[0] = x1
    h2 = _ln(x1) * (1.0 + mod_ref[0, 0, 4:5, :]) + mod_ref[0, 0, 3:4, :]
    hi = h2.astype(BF16)
    h2_ref[0] = hi
    lo = (h2 - hi.astype(F32)).astype(BF16)
    lg_ref[0] = jnp.dot(jnp.concatenate([hi, lo, hi], axis=1), wr_ref[0], preferred_element_type=F32)


def _split_router(w_router):
    w = jnp.pad(w_router, ((0, 0), (0, 0), (0, LANES - w_router.shape[-1])))
    hi = w.astype(BF16)
    lo = (w - hi.astype(F32)).astype(BF16)
    return jnp.concatenate([hi, hi, lo], axis=1)


def _mixout(att, ml, gm, x, mod, w_out, ln_g, ln_b, w_router3, alpha, l):
    b, s, d = x.shape
    tm = 512
    row = lambda w: pl.BlockSpec((1, tm, w), lambda i, j: (i, j, 0))
    full = lambda a: pl.BlockSpec(a.shape, lambda i, j: (0,) * a.ndim)
    layer = lambda a: pl.BlockSpec((1,) + a.shape[1:], lambda i, j: (l,) + (0,) * (a.ndim - 1))
    return pl.pallas_call(
        functools.partial(_mixout_kernel, alpha=alpha),
        grid=(b, s // tm),
        in_specs=[row(512), row(256), row(256), row(d),
                  pl.BlockSpec((1, 1, 6, d), lambda i, j: (l, i, 0, 0)),
                  layer(w_out), full(ln_g), full(ln_b), layer(w_router3)],
        out_specs=[row(d), row(d), row(LANES)],
        out_shape=[jax.ShapeDtypeStruct((b, s, d), F32),
                   jax.ShapeDtypeStruct((b, s, d), BF16),
                   jax.ShapeDtypeStruct((b, s, LANES), F32)],
        compiler_params=_cparams("parallel", "parallel"),
        name="mixout",
    )(att, ml, gm, x, mod, w_out, ln_g, ln_b, w_router3)


def _route_kernel(lg_ref, pos_ref, aff_ref, *, cap, ne, nc):
    L = CHUNK
    lane = lax.broadcasted_iota(jnp.int32, (1, LANES), 1)
    l = jnp.where(lane < ne, lg_ref[0], -jnp.inf)
    e = jnp.exp(l - jnp.max(l, axis=1, keepdims=True))
    aff = e / jnp.sum(e, axis=1, keepdims=True)
    aff_ref[0] = aff
    bits = pltpu.bitcast(aff, jnp.int32)

    def bit_body(k, thr):
        cand = jnp.bitwise_or(thr, jnp.left_shift(jnp.int32(1), 30 - k))
        cnt = jnp.sum((bits >= cand).astype(jnp.int32), axis=0, keepdims=True)
        return jnp.where(cnt >= cap, cand, thr)

    thr = lax.fori_loop(0, 31, bit_body, jnp.zeros((1, LANES), jnp.int32))
    n_gt = jnp.sum((bits > thr).astype(jnp.int32), axis=0, keepdims=True)
    need = (cap - n_gt).astype(F32)

    it = lax.broadcasted_iota(jnp.int32, (L, L), 0)
    iu = lax.broadcasted_iota(jnp.int32, (L, L), 1)
    before = (iu < it).astype(BF16)

    def chunk_body(c, carry):
        run_eq, run_sel = carry
        r0 = pl.multiple_of(c * L, L)
        bc = pltpu.bitcast(aff_ref[0, pl.ds(r0, L), :], jnp.int32)
        eq = bc == thr
        eqf = jnp.where(eq, 1.0, 0.0)
        eq_before = jnp.dot(before, eqf.astype(BF16), preferred_element_type=F32) + run_eq
        sel = jnp.logical_or(bc > thr, jnp.logical_and(eq, eq_before < need))
        self_ = jnp.where(sel, 1.0, 0.0)
        sel_before = jnp.dot(before, self_.astype(BF16), preferred_element_type=F32) + run_sel
        pos_ref[0, pl.ds(r0, L), :] = jnp.where(sel, sel_before.astype(jnp.int32), -1)
        return (run_eq + jnp.sum(eqf, axis=0, keepdims=True),
                run_sel + jnp.sum(self_, axis=0, keepdims=True))

    zero = jnp.zeros((1, LANES), F32)
    lax.fori_loop(0, nc, chunk_body, (zero, zero))


def _route(logits, ne, cap):
    b, s, _ = logits.shape
    blk = pl.BlockSpec((1, s, LANES), lambda i: (i, 0, 0))
    return pl.pallas_call(
        functools.partial(_route_kernel, cap=cap, ne=ne, nc=s // CHUNK),
        grid=(b,),
        in_specs=[blk],
        out_specs=[blk, blk],
        out_shape=[jax.ShapeDtypeStruct((b, s, LANES), jnp.int32),
                   jax.ShapeDtypeStruct((b, s, LANES), F32)],
        compiler_params=_cparams("parallel"),
        name="route",
    )(logits)


def _ffn_kernel(pos_ref, aff_ref, h2_ref, wg_ref, wu_ref, wd_ref, o_ref, acc_scr, *, cap, tf):
    s = h2_ref.shape[1]
    slot = lax.broadcasted_iota(jnp.int32, (cap, s), 0)
    hit = pos_ref[0, 0] == slot
    onehot = jnp.where(hit, 1.0, 0.0).astype(BF16)
    gate = jnp.sum(jnp.where(hit, aff_ref[0, 0], 0.0), axis=1, keepdims=True)
    xg = jnp.dot(onehot, h2_ref[0], preferred_element_type=F32).astype(BF16)
    f = wg_ref.shape[3]
    for k in range(f // tf):
        cols = slice(k * tf, (k + 1) * tf)
        a = jnp.dot(xg, wg_ref[0, 0, :, cols], preferred_element_type=F32)
        u = jnp.dot(xg, wu_ref[0, 0, :, cols], preferred_element_type=F32)
        hid = (_silu(a) * u).astype(BF16)
        part = jnp.dot(hid, wd_ref[0, 0, cols, :], preferred_element_type=F32)
        if k == 0:
            acc_scr[...] = part
        else:
            acc_scr[...] += part
    o_ref[0] = (acc_scr[...] * gate).astype(o_ref.dtype)


def _expert_ffn(pos_t, aff_t, h2, w_gate, w_up, w_down, cap, l):
    b, s, d = h2.shape
    _, ne, _, f = w_gate.shape
    return pl.pallas_call(
        functools.partial(_ffn_kernel, cap=cap, tf=512),
        grid=(ne, b),
        in_specs=[pl.BlockSpec((1, 1, 1, s), lambda e, i: (i, e, 0, 0)),
                  pl.BlockSpec((1, 1, 1, s), lambda e, i: (i, e, 0, 0)),
                  pl.BlockSpec((1, s, d), lambda e, i: (i, 0, 0)),
                  pl.BlockSpec((1, 1, d, f), lambda e, i: (l, e, 0, 0)),
                  pl.BlockSpec((1, 1, d, f), lambda e, i: (l, e, 0, 0)),
                  pl.BlockSpec((1, 1, f, d), lambda e, i: (l, e, 0, 0))],
        out_specs=pl.BlockSpec((1, cap, d), lambda e, i: (i, e, 0)),
        out_shape=jax.ShapeDtypeStruct((b, ne * cap, d), BF16),
        scratch_shapes=[pltpu.VMEM((cap, d), F32)],
        compiler_params=_cparams("parallel", "parallel"),
        name="expert_ffn",
    )(pos_t, aff_t, h2, w_gate, w_up, w_down)


def _scatter_kernel(pos_ref, ye_ref, x1_ref, mod_ref, lng_ref, lnb_ref, o_ref, *, cap, ne, alpha):
    ts = pos_ref.shape[1]
    slot = lax.broadcasted_iota(jnp.int32, (ts, cap), 1)
    pos = pos_ref[0]
    onehot = jnp.concatenate(
        [jnp.where(pos[:, e:e + 1] == slot, 1.0, 0.0).astype(BF16) for e in range(ne)], axis=1)
    y = jnp.dot(onehot, ye_ref[0], preferred_element_type=F32)
    y = alpha * x1_ref[0] + (1.0 + mod_ref[0, 0, 5:6, :]) * y
    o_ref[0] = _ln(y) * lng_ref[...] + lnb_ref[...]


def _scatter(pos, ye, x1, mod, ln_g, ln_b, alpha, ne, l):
    b, s, d = x1.shape
    cap = ye.shape[1] // ne
    ts = 512
    return pl.pallas_call(
        functools.partial(_scatter_kernel, cap=cap, ne=ne, alpha=alpha),
        grid=(b, s // ts),
        in_specs=[pl.BlockSpec((1, ts, LANES), lambda i, j: (i, j, 0)),
                  pl.BlockSpec((1, ne * cap, d), lambda i, j: (i, 0, 0)),
                  pl.BlockSpec((1, ts, d), lambda i, j: (i, j, 0)),
                  pl.BlockSpec((1, 1, 6, d), lambda i, j: (l, i, 0, 0)),
                  pl.BlockSpec((1, d), lambda i, j: (0, 0)),
                  pl.BlockSpec((1, d), lambda i, j: (0, 0))],
        out_specs=pl.BlockSpec((1, ts, d), lambda i, j: (i, j, 0)),
        out_shape=jax.ShapeDtypeStruct((b, s, d), F32),
        compiler_params=_cparams("parallel", "parallel"),
        name="scatter_ln",
    )(pos, ye, x1, mod, ln_g, ln_b)


def kernel(x, c, w_ada, b_ada, w_in, conv_w, conv_b, gate_b, sink, rel_bias, w_s, b_s, w_out, w_router,
           w_gate, w_up, w_down, ln_g, ln_b):
    depth = w_ada.shape[0]
    b, s, d = x.shape
    ne = w_router.shape[-1]
    cap = max(1, min(s, EC_CAPACITY_FACTOR * s // ne))
    alpha = float((2 * depth) ** 0.25)

    mod = _modulation(c, w_ada, b_ada).reshape(depth, b, 6, d)
    bias = _band_bias(rel_bias)
    w_in_r = _rearrange_w_in(w_in)
    w_router3 = _split_router(w_router)
    w_out_h, w_gate_h, w_up_h, w_down_h = (w.astype(BF16) for w in (w_out, w_gate, w_up, w_down))

    for l in range(depth):
        att, mqk, mv, mo, gg, mg = _projection(x, mod, w_in_r, l)
        att_o = _attention(att, sink[l], bias)
        ml_o = _mlstm(mqk, mv, mo, mg, gate_b[l], conv_w[l], conv_b[l])
        gm_o = _gmlp(gg, w_s[l], b_s[l])
        x1, h2, logits = _mixout(att_o, ml_o, gm_o, x, mod, w_out_h, ln_g[l, 0:1], ln_b[l, 0:1],
                                 w_router3, alpha, l)
        pos, aff = _route(logits, ne, cap)
        pos_t = pos[:, :, :ne].transpose(0, 2, 1).reshape(b, ne, 1, s)
        aff_t = aff[:, :, :ne].transpose(0, 2, 1).reshape(b, ne, 1, s)
        ye = _expert_ffn(pos_t, aff_t, h2, w_gate_h, w_up_h, w_down_h, cap, l)
        x = _scatter(pos, ye, x1, mod, ln_g[l, 1:2], ln_b[l, 1:2], alpha, ne, l)
    return x
```

```python
import functools

import numpy as np
import jax
import jax.numpy as jnp
from jax import lax
from jax.experimental import pallas as pl
from jax.experimental.pallas import tpu as pltpu

F32 = jnp.float32
BF16 = jnp.bfloat16
HIGHEST = lax.Precision.HIGHEST

HEAD_DIM = 64
LANES = 128
CHUNK = 128
N_ATT_HEADS = 8
N_KV_HEADS = 2
N_ML_HEADS = 4
N_GM_GROUPS = 4
ML_CONV = 3
N_REL_BUCKETS = 32
REL_MAX_DIST = 128
WINDOW = 128
EC_CAPACITY_FACTOR = 2
LN_EPS = 1e-5
NEG_INF = -1e30
VMEM_LIMIT = 56 * 1024 * 1024

NT_DIMS = (((1,), (1,)), ((), ()))
TN_DIMS = (((0,), (0,)), ((), ()))


def _cparams(*sem):
    return pltpu.CompilerParams(dimension_semantics=sem, vmem_limit_bytes=VMEM_LIMIT)


def _ln(x):
    mu = jnp.mean(x, axis=-1, keepdims=True)
    xc = x - mu
    var = jnp.mean(xc * xc, axis=-1, keepdims=True)
    return xc * lax.rsqrt(var + LN_EPS)


def _gelu(x):
    return 0.5 * x * (1.0 + lax.erf(x * (2.0 ** -0.5)))


def _silu(x):
    return x * jax.nn.sigmoid(x)


def _mod_kernel(c_ref, w_ref, b_ref, o_ref):
    sc = _silu(c_ref[...])
    o_ref[0] = jnp.dot(sc, w_ref[0], precision=HIGHEST, preferred_element_type=F32) + b_ref[0]


def _modulation(c, w_ada, b_ada):
    depth, d, n = w_ada.shape
    b = c.shape[0]
    tn = 1536
    return pl.pallas_call(
        _mod_kernel,
        grid=(depth, n // tn),
        in_specs=[pl.BlockSpec((b, d), lambda l, j: (0, 0)),
                  pl.BlockSpec((1, d, tn), lambda l, j: (l, 0, j)),
                  pl.BlockSpec((1, 1, tn), lambda l, j: (l, 0, j))],
        out_specs=pl.BlockSpec((1, b, tn), lambda l, j: (l, 0, j)),
        out_shape=jax.ShapeDtypeStruct((depth, b, n), F32),
        compiler_params=_cparams("parallel", "parallel"),
        name="modulation",
    )(c, w_ada, b_ada.reshape(depth, 1, n))


def _t5_buckets(rel):
    nb = N_REL_BUCKETS // 2
    max_exact = nb // 2
    ret = np.where(rel > 0, nb, 0)
    n = np.abs(rel)
    large = max_exact + (np.log(np.maximum(n, 1) / max_exact) / np.log(REL_MAX_DIST / max_exact)
                         * (nb - max_exact)).astype(np.int32)
    large = np.minimum(large, nb - 1)
    return (ret + np.where(n < max_exact, n, large)).astype(np.int32)


def _bias_kernel(rb_ref, bk_ref, mk_ref, o_ref):
    bk = bk_ref[...]
    inband = mk_ref[...] != 0
    for h in range(N_ATT_HEADS):
        acc = jnp.zeros(bk.shape, F32)
        for k in range(N_REL_BUCKETS):
            acc = jnp.where(bk == k, rb_ref[k, h], acc)
        o_ref[h] = jnp.where(inband, acc, NEG_INF)


def _band_bias(rel_bias):
    qi = np.arange(CHUNK)[:, None]
    sj = np.arange(3 * CHUNK)[None, :]
    rel = sj - CHUNK - qi
    buckets = jnp.asarray(_t5_buckets(rel), jnp.int32)
    inband = jnp.asarray((np.abs(rel) <= WINDOW).astype(np.int32))
    return pl.pallas_call(
        _bias_kernel,
        in_specs=[pl.BlockSpec(memory_space=pltpu.SMEM),
                  pl.BlockSpec((CHUNK, 3 * CHUNK), lambda: (0, 0)),
                  pl.BlockSpec((CHUNK, 3 * CHUNK), lambda: (0, 0))],
        out_specs=pl.BlockSpec((N_ATT_HEADS, CHUNK, 3 * CHUNK), lambda: (0, 0, 0)),
        out_shape=jax.ShapeDtypeStruct((N_ATT_HEADS, CHUNK, 3 * CHUNK), F32),
        name="band_bias",
    )(rel_bias, buckets, inband)


ATT_COLS = 1024
MQK_COLS = 512
MV_COLS = 256
MO_COLS = 256
GG_COLS = 512
MG_COLS = 128
PROJ_PIECES = (ATT_COLS, MQK_COLS, MV_COLS, MO_COLS, GG_COLS, MG_COLS)
PROJ_DTYPES = (BF16, F32, BF16, F32, F32, F32)


def _rearrange_w_in(w_in):
    q = w_in[..., 0:512]
    k0, k1 = w_in[..., 512:576], w_in[..., 576:640]
    v0, v1 = w_in[..., 640:704], w_in[..., 704:768]
    rest = w_in[..., 768:1792]
    mg = w_in[..., 1792:1808]
    gg = w_in[..., 1808:2320]
    pad = jnp.zeros(w_in.shape[:-1] + (MG_COLS - mg.shape[-1],), w_in.dtype)
    return jnp.concatenate([q, k0, k0, k1, k1, v0, v0, v1, v1, rest, gg, mg, pad], axis=-1).astype(BF16)


def _proj_kernel(x_ref, mod_ref, w_ref, *rest):
    out_refs, h_scr = rest[:-1], rest[-1]
    h = _ln(x_ref[0]) * (1.0 + mod_ref[0, 0, 1:2, :]) + mod_ref[0, 0, 0:1, :]
    h_scr[...] = h.astype(BF16)
    off = 0
    for width, o_ref in zip(PROJ_PIECES, out_refs):
        o_ref[0] = jnp.dot(h_scr[...], w_ref[0, :, off:off + width],
                           preferred_element_type=F32).astype(o_ref.dtype)
        off += width


def _projection(x, mod, w_in_r, l):
    b, s, d = x.shape
    tm = 512
    n = w_in_r.shape[-1]
    return pl.pallas_call(
        _proj_kernel,
        grid=(b, s // tm),
        in_specs=[pl.BlockSpec((1, tm, d), lambda i, j: (i, j, 0)),
                  pl.BlockSpec((1, 1, 6, d), lambda i, j: (l, i, 0, 0)),
                  pl.BlockSpec((1, d, n), lambda i, j: (l, 0, 0))],
        out_specs=[pl.BlockSpec((1, tm, w), lambda i, j: (i, j, 0)) for w in PROJ_PIECES],
        out_shape=[jax.ShapeDtypeStruct((b, s, w), dt) for w, dt in zip(PROJ_PIECES, PROJ_DTYPES)],
        scratch_shapes=[pltpu.VMEM((tm, d), BF16)],
        compiler_params=_cparams("parallel", "parallel"),
        name="proj",
    )(x, mod, w_in_r)


def _attn_kernel(sink_ref, q_ref, kv_ref, bias_ref, o_ref, *, nb, tq):
    t = pl.program_id(1)
    lane = lax.broadcasted_iota(jnp.int32, (1, LANES), 1)
    lo = lane < HEAD_DIM
    col = lax.broadcasted_iota(jnp.int32, (1, 3 * CHUNK), 1)
    scale = HEAD_DIM ** -0.5
    for j in range(tq // CHUNK):
        n = t * (tq // CHUNK) + j
        ps = pl.multiple_of(jnp.maximum(n - 1, 0) * CHUNK, CHUNK)
        cs = pl.multiple_of(n * CHUNK, CHUNK)
        ns = pl.multiple_of(jnp.minimum(n + 1, nb - 1) * CHUNK, CHUNK)
        kvb = jnp.concatenate([kv_ref[0, pl.ds(ps, CHUNK), :],
                               kv_ref[0, pl.ds(cs, CHUNK), :],
                               kv_ref[0, pl.ds(ns, CHUNK), :]], axis=0)
        valid = jnp.logical_and(jnp.logical_or(col >= CHUNK, n > 0),
                                jnp.logical_or(col < 2 * CHUNK, n < nb - 1))
        q = q_ref[0, j * CHUNK:(j + 1) * CHUNK, :]
        outs = []
        for grp in range(N_ATT_HEADS // 2):
            kvh = grp // 2
            kd = kvb[:, kvh * LANES:(kvh + 1) * LANES]
            vd = kvb[:, (2 + kvh) * LANES:(3 + kvh) * LANES]
            qg = q[:, grp * LANES:(grp + 1) * LANES]
            zq = jnp.zeros_like(qg)
            q2 = jnp.concatenate([jnp.where(lo, qg, zq), jnp.where(lo, zq, qg)], axis=0)
            logits = lax.dot_general(q2, kd, NT_DIMS, preferred_element_type=F32) * scale
            acc = None
            for g in range(2):
                h = grp * 2 + g
                l = logits[g * CHUNK:(g + 1) * CHUNK] + bias_ref[h]
                l = jnp.where(valid, l, NEG_INF)
                sk = sink_ref[h]
                m = jnp.maximum(jnp.max(l, axis=1, keepdims=True), sk)
                p = jnp.exp(l - m)
                den = jnp.sum(p, axis=1, keepdims=True) + jnp.exp(sk - m)
                zv = jnp.zeros_like(vd)
                vh = jnp.where(lo, vd, zv) if g == 0 else jnp.where(lo, zv, vd)
                part = jnp.dot(p.astype(BF16), vh, preferred_element_type=F32) / den
                acc = part if acc is None else acc + part
            outs.append(acc)
        o_ref[0, j * CHUNK:(j + 1) * CHUNK, :] = jnp.concatenate(outs, axis=1).astype(o_ref.dtype)


def _attention(att, sink, bias):
    b, s, _ = att.shape
    tq = 512
    nb = s // CHUNK
    return pl.pallas_call(
        functools.partial(_attn_kernel, nb=nb, tq=tq),
        grid=(b, s // tq),
        in_specs=[pl.BlockSpec(memory_space=pltpu.SMEM),
                  pl.BlockSpec((1, tq, 512), lambda i, j: (i, j, 0)),
                  pl.BlockSpec((1, s, 512), lambda i, j: (i, 0, 1)),
                  pl.BlockSpec((N_ATT_HEADS, CHUNK, 3 * CHUNK), lambda i, j: (0, 0, 0))],
        out_specs=pl.BlockSpec((1, tq, 512), lambda i, j: (i, j, 0)),
        out_shape=jax.ShapeDtypeStruct((b, s, 512), BF16),
        compiler_params=_cparams("parallel", "parallel"),
        name="attn",
    )(sink, att, att, bias)


def _mlstm_kernel(mqk_ref, mv_ref, mo_ref, g4_ref, gb_ref, cw_ref, cb_ref, o_ref,
                  qk_scr, rrow_scr, rcol_scr, hf_scr, hb_scr, st_scr, m_scr, *, nc):
    L = CHUNK
    s_len = nc * L
    ng = 4 * N_ML_HEADS
    lane = lax.broadcasted_iota(jnp.int32, (1, LANES), 1)
    lo = lane < HEAD_DIM

    w0, w1, w2, cb = cw_ref[0:1, :], cw_ref[1:2, :], cw_ref[2:3, :], cb_ref[...]
    row = lax.broadcasted_iota(jnp.int32, (L, 1), 0)
    lane_qk = lax.broadcasted_iota(jnp.int32, (1, 4 * LANES), 1)
    kscale = jnp.where(lane_qk >= 2 * LANES, HEAD_DIM ** -0.5, 1.0).astype(F32)

    def conv_body(c, carry):
        r0 = pl.multiple_of(c * L, L)
        xc = mqk_ref[0, pl.ds(r0, L), :]
        pr = mqk_ref[0, pl.ds(jnp.maximum(r0 - 1, 0), 1), :]
        nx = mqk_ref[0, pl.ds(jnp.minimum(r0 + L, s_len - 1), 1), :]
        pr = jnp.where(c > 0, pr, 0.0)
        nx = jnp.where(c < nc - 1, nx, 0.0)
        xp = jnp.where(row == 0, pr, pltpu.roll(xc, 1, 0))
        xn = jnp.where(row == L - 1, nx, pltpu.roll(xc, L - 1, 0))
        y = w0 * xp + w1 * xc + w2 * xn + cb
        qk_scr[pl.ds(r0, L), :] = (_silu(y) * kscale).astype(BF16)
        return carry

    lax.fori_loop(0, nc, conv_body, 0)

    g = (g4_ref[0] + gb_ref[...]).reshape(nc * ng, L)
    kind = (lax.broadcasted_iota(jnp.int32, (nc * ng, 1), 0) % ng) // N_ML_HEADS
    logf = jnp.minimum(g, 0.0) - jnp.log1p(jnp.exp(-jnp.abs(g)))
    x = jnp.where(jnp.logical_or(kind == 1, kind == 3), logf, 0.0)
    iu = lax.broadcasted_iota(jnp.int32, (L, L), 0)
    it = lax.broadcasted_iota(jnp.int32, (L, L), 1)
    prefix = jnp.dot(x, (iu <= it).astype(F32), precision=HIGHEST, preferred_element_type=F32)
    suffix = jnp.dot(x, (iu >= it).astype(F32), precision=HIGHEST, preferred_element_type=F32)
    r = jnp.where(kind == 1, prefix, jnp.where(kind == 3, suffix, g))
    rrow_scr[...] = r.reshape(nc, ng, L)
    zpad = jnp.zeros((L - ng, L), F32)
    for c in range(nc):
        rcol_scr[c] = jnp.concatenate([r[c * ng:(c + 1) * ng], zpad], axis=0).T

    st_scr[...] = jnp.zeros(st_scr.shape, F32)
    m_scr[...] = jnp.zeros(m_scr.shape, F32)

    ones_blk = jnp.ones((L, LANES), BF16)

    def chunk_dir(c, dirn, h_scr):
        r0 = pl.multiple_of(c * L, L)
        qk = qk_scr[pl.ds(r0, L), :]
        vrows = mv_ref[0, pl.ds(r0, L), :]
        rr = rrow_scr[c]
        rc = rcol_scr[c]
        causal = (it <= iu) if dirn == 0 else (it >= iu)
        for grp in range(N_ML_HEADS // 2):
            qg = qk[:, grp * LANES:(grp + 1) * LANES]
            kg = qk[:, (2 + grp) * LANES:(3 + grp) * LANES]
            vg = vrows[:, grp * LANES:(grp + 1) * LANES]
            zb = jnp.zeros_like(qg)
            hv = []
            for odd in range(2):
                h = grp * 2 + odd
                hm = lo if odd == 0 else jnp.logical_not(lo)
                qh = jnp.where(hm, qg, zb)
                kh = jnp.where(hm, kg, zb)
                vext = jnp.concatenate([jnp.where(hm, vg, zb), ones_blk], axis=1)
                ii, bi = dirn * 2 * N_ML_HEADS + h, dirn * 2 * N_ML_HEADS + N_ML_HEADS + h
                i_row, b_row = rr[ii:ii + 1, :], rr[bi:bi + 1, :]
                i_col, b_col = rc[:, ii:ii + 1], rc[:, bi:bi + 1]
                b_last = b_row[:, L - 1:L] if dirn == 0 else b_row[:, 0:1]
                sidx = dirn * N_ML_HEADS + h
                m_old = m_scr[sidx][:, 0:1]
                st = st_scr[sidx]
                a_row = i_row - b_row
                a_max = jnp.max(jnp.where(causal, a_row, -jnp.inf), axis=1, keepdims=True)
                m_t = jnp.maximum(b_col + m_old, b_col + a_max)
                u = jnp.broadcast_to(b_col - m_t, (L, L))
                smat = (lax.dot_general(qh, kg, NT_DIMS, preferred_element_type=F32)
                        * jnp.exp(jnp.where(causal, u + a_row, -jnp.inf)))
                iw = jnp.concatenate([jnp.exp(u + m_old)] * 2, axis=1)
                num = (jnp.dot(smat.astype(BF16), vext, preferred_element_type=F32)
                       + iw * jnp.dot(qh, st.astype(BF16), preferred_element_type=F32))
                hv.append(num[:, :LANES] / jnp.maximum(jnp.abs(num[:, LANES:]), jnp.exp(-m_t)))
                ws_col = b_last - b_col + i_col
                ws_row = b_last - b_row + i_row
                m_new = jnp.maximum(b_last + m_old, jnp.max(ws_row, axis=1, keepdims=True))
                wv = (jnp.exp(ws_col - m_new) * vext.astype(F32)).astype(BF16)
                decay = jnp.exp(b_last + m_old - m_new)
                st_scr[sidx] = decay * st + lax.dot_general(kh, wv, TN_DIMS, preferred_element_type=F32)
                m_scr[sidx] = jnp.broadcast_to(m_new, (1, LANES))
            h_scr[pl.ds(r0, L), grp * LANES:(grp + 1) * LANES] = jnp.where(lo, hv[0], hv[1])

    def body(i, carry):
        chunk_dir(i, 0, hf_scr)
        chunk_dir(nc - 1 - i, 1, hb_scr)
        return carry

    lax.fori_loop(0, nc, body, 0)
    o_ref[0] = (jax.nn.sigmoid(mo_ref[0]) * (hf_scr[...] + hb_scr[...])).astype(o_ref.dtype)


def _mlstm(mqk, mv, mo, mg, gate_b, conv_w, conv_b):
    b, s, _ = mqk.shape
    nc = s // CHUNK
    ng = 4 * N_ML_HEADS
    g4 = mg[:, :, :ng].reshape(b, nc, CHUNK, ng).transpose(0, 1, 3, 2)
    seq = lambda w: pl.BlockSpec((1, s, w), lambda i: (i, 0, 0))
    return pl.pallas_call(
        functools.partial(_mlstm_kernel, nc=nc),
        grid=(b,),
        in_specs=[seq(512), seq(256), seq(256),
                  pl.BlockSpec((1, nc, ng, CHUNK), lambda i: (i, 0, 0, 0)),
                  pl.BlockSpec((ng, 1), lambda i: (0, 0)),
                  pl.BlockSpec((ML_CONV, 512), lambda i: (0, 0)),
                  pl.BlockSpec((1, 512), lambda i: (0, 0))],
        out_specs=seq(256),
        out_shape=jax.ShapeDtypeStruct((b, s, 256), BF16),
        scratch_shapes=[pltpu.VMEM((s, 512), BF16),
                        pltpu.VMEM((nc, ng, CHUNK), F32),
                        pltpu.VMEM((nc, CHUNK, LANES), F32),
                        pltpu.VMEM((s, 256), F32),
                        pltpu.VMEM((s, 256), F32),
                        pltpu.VMEM((2 * N_ML_HEADS, LANES, 2 * LANES), F32),
                        pltpu.VMEM((2 * N_ML_HEADS, 1, LANES), F32)],
        compiler_params=_cparams("parallel"),
        name="mlstm",
    )(mqk, mv, mo, g4, gate_b.reshape(ng, 1), conv_w, conv_b.reshape(1, 512))


def _gmlp_kernel(gg_ref, ws_ref, bs_ref, o_ref, *, tg):
    lane = lax.broadcasted_iota(jnp.int32, (1, 2 * LANES), 1)
    for j in range(tg // CHUNK):
        rows = slice(j * CHUNK, (j + 1) * CHUNK)
        u = _gelu(gg_ref[0, rows, 0:256])
        v = _ln(_gelu(gg_ref[0, rows, 256:512])).astype(BF16)
        zv = jnp.zeros_like(v)
        s = None
        for g in range(N_GM_GROUPS):
            ing = jnp.logical_and(lane >= g * HEAD_DIM, lane < (g + 1) * HEAD_DIM)
            part = jnp.dot(ws_ref[g], jnp.where(ing, v, zv), preferred_element_type=F32)
            part = part + jnp.where(ing, bs_ref[:, g:g + 1], 0.0)
            s = part if s is None else s + part
        o_ref[0, rows, :] = (u * s).astype(o_ref.dtype)


def _gmlp(gg, w_s, b_s):
    b, s, _ = gg.shape
    tg = 512
    return pl.pallas_call(
        functools.partial(_gmlp_kernel, tg=tg),
        grid=(b, s // tg),
        in_specs=[pl.BlockSpec((1, tg, 512), lambda i, j: (i, j, 0)),
                  pl.BlockSpec((N_GM_GROUPS, CHUNK, CHUNK), lambda i, j: (0, 0, 0)),
                  pl.BlockSpec((CHUNK, N_GM_GROUPS), lambda i, j: (0, 0))],
        out_specs=pl.BlockSpec((1, tg, 256), lambda i, j: (i, j, 0)),
        out_shape=jax.ShapeDtypeStruct((b, s, 256), BF16),
        compiler_params=_cparams("parallel", "parallel"),
        name="gmlp",
    )(gg, w_s.astype(BF16), b_s.T)


def _mixout_kernel(att_ref, ml_ref, gm_ref, x_ref, mod_ref, wo_ref, lng_ref, lnb_ref, wr_ref,
                   x1_ref, h2_ref, lg_ref, *, alpha):
    mix = jnp.dot(att_ref[0], wo_ref[0, 0:512, :], preferred_element_type=F32)
    mix = mix + jnp.dot(ml_ref[0], wo_ref[0, 512:768, :], preferred_element_type=F32)
    mix = mix + jnp.dot(gm_ref[0], wo_ref[0, 768:1024, :], preferred_element_type=F32)
    y = alpha * x_ref[0] + (1.0 + mod_ref[0, 0, 2:3, :]) * mix
    x1 = _ln(y) * lng_ref[...] + lnb_ref[...]
    x1_ref[0] = x1
    h2 = _ln(x1) * (1.0 + mod_ref[0, 0, 4:5, :]) + mod_ref[0, 0, 3:4, :]
    hi = h2.astype(BF16)
    h2_ref[0] = hi
    lo = (h2 - hi.astype(F32)).astype(BF16)
    lg_ref[0] = jnp.dot(jnp.concatenate([hi, lo, hi], axis=1), wr_ref[0], preferred_element_type=F32)


def _split_router(w_router):
    w = jnp.pad(w_router, ((0, 0), (0, 0), (0, LANES - w_router.shape[-1])))
    hi = w.astype(BF16)
    lo = (w - hi.astype(F32)).astype(BF16)
    return jnp.concatenate([hi, hi, lo], axis=1)


def _mixout(att, ml, gm, x, mod, w_out, ln_g, ln_b, w_router3, alpha, l):
    b, s, d = x.shape
    tm = 512
    row = lambda w: pl.BlockSpec((1, tm, w), lambda i, j: (i, j, 0))
    full = lambda a: pl.BlockSpec(a.shape, lambda i, j: (0,) * a.ndim)
    layer = lambda a: pl.BlockSpec((1,) + a.shape[1:], lambda i, j: (l,) + (0,) * (a.ndim - 1))
    return pl.pallas_call(
        functools.partial(_mixout_kernel, alpha=alpha),
        grid=(b, s // tm),
        in_specs=[row(512), row(256), row(256), row(d),
                  pl.BlockSpec((1, 1, 6, d), lambda i, j: (l, i, 0, 0)),
                  layer(w_out), full(ln_g), full(ln_b), layer(w_router3)],
        out_specs=[row(d), row(d), row(LANES)],
        out_shape=[jax.ShapeDtypeStruct((b, s, d), F32),
                   jax.ShapeDtypeStruct((b, s, d), BF16),
                   jax.ShapeDtypeStruct((b, s, LANES), F32)],
        compiler_params=_cparams("parallel", "parallel"),
        name="mixout",
    )(att, ml, gm, x, mod, w_out, ln_g, ln_b, w_router3)


def _route_kernel(lg_ref, pos_ref, aff_ref, *, cap, ne, nc):
    L = CHUNK
    lane = lax.broadcasted_iota(jnp.int32, (1, LANES), 1)
    l = jnp.where(lane < ne, lg_ref[0], -jnp.inf)
    e = jnp.exp(l - jnp.max(l, axis=1, keepdims=True))
    aff = e / jnp.sum(e, axis=1, keepdims=True)
    aff_ref[0] = aff
    bits = pltpu.bitcast(aff, jnp.int32)

    def bit_body(k, thr):
        cand = jnp.bitwise_or(thr, jnp.left_shift(jnp.int32(1), 30 - k))
        cnt = jnp.sum((bits >= cand).astype(jnp.int32), axis=0, keepdims=True)
        return jnp.where(cnt >= cap, cand, thr)

    thr = lax.fori_loop(0, 31, bit_body, jnp.zeros((1, LANES), jnp.int32))
    n_gt = jnp.sum((bits > thr).astype(jnp.int32), axis=0, keepdims=True)
    need = (cap - n_gt).astype(F32)

    it = lax.broadcasted_iota(jnp.int32, (L, L), 0)
    iu = lax.broadcasted_iota(jnp.int32, (L, L), 1)
    before = (iu < it).astype(BF16)

    run_eq = run_sel = jnp.zeros((1, LANES), F32)
    for c in range(nc):
        rows = slice(c * L, (c + 1) * L)
        bc = pltpu.bitcast(aff_ref[0, rows, :], jnp.int32)
        eq = bc == thr
        eqf = jnp.where(eq, 1.0, 0.0)
        eq_before = jnp.dot(before, eqf.astype(BF16), preferred_element_type=F32) + run_eq
        sel = jnp.logical_or(bc > thr, jnp.logical_and(eq, eq_before < need))
        self_ = jnp.where(sel, 1.0, 0.0)
        sel_before = jnp.dot(before, self_.astype(BF16), preferred_element_type=F32) + run_sel
        pos_ref[0, rows, :] = jnp.where(sel, sel_before.astype(jnp.int32), -1)
        run_eq = run_eq + jnp.sum(eqf, axis=0, keepdims=True)
        run_sel = run_sel + jnp.sum(self_, axis=0, keepdims=True)


def _route(logits, ne, cap):
    b, s, _ = logits.shape
    blk = pl.BlockSpec((1, s, LANES), lambda i: (i, 0, 0))
    return pl.pallas_call(
        functools.partial(_route_kernel, cap=cap, ne=ne, nc=s // CHUNK),
        grid=(b,),
        in_specs=[blk],
        out_specs=[blk, blk],
        out_shape=[jax.ShapeDtypeStruct((b, s, LANES), jnp.int32),
                   jax.ShapeDtypeStruct((b, s, LANES), F32)],
        compiler_params=_cparams("parallel"),
        name="route",
    )(logits)


def _ffn_kernel(pos_ref, aff_ref, h2_ref, wg_ref, wu_ref, wd_ref, o_ref, acc_scr, *, cap, tf):
    s = h2_ref.shape[1]
    slot = lax.broadcasted_iota(jnp.int32, (cap, s), 0)
    hit = pos_ref[0, 0] == slot
    onehot = jnp.where(hit, 1.0, 0.0).astype(BF16)
    gate = jnp.sum(jnp.where(hit, aff_ref[0, 0], 0.0), axis=1, keepdims=True)
    xg = jnp.dot(onehot, h2_ref[0], preferred_element_type=F32).astype(BF16)
    f = wg_ref.shape[3]
    for k in range(f // tf):
        cols = slice(k * tf, (k + 1) * tf)
        a = jnp.dot(xg, wg_ref[0, 0, :, cols], preferred_element_type=F32)
        u = jnp.dot(xg, wu_ref[0, 0, :, cols], preferred_element_type=F32)
        hid = (_silu(a) * u).astype(BF16)
        part = jnp.dot(hid, wd_ref[0, 0, cols, :], preferred_element_type=F32)
        if k == 0:
            acc_scr[...] = part
        else:
            acc_scr[...] += part
    o_ref[0] = (acc_scr[...] * gate).astype(o_ref.dtype)


def _expert_ffn(pos_t, aff_t, h2, w_gate, w_up, w_down, cap, l):
    b, s, d = h2.shape
    _, ne, _, f = w_gate.shape
    return pl.pallas_call(
        functools.partial(_ffn_kernel, cap=cap, tf=512),
        grid=(ne, b),
        in_specs=[pl.BlockSpec((1, 1, 1, s), lambda e, i: (i, e, 0, 0)),
                  pl.BlockSpec((1, 1, 1, s), lambda e, i: (i, e, 0, 0)),
                  pl.BlockSpec((1, s, d), lambda e, i: (i, 0, 0)),
                  pl.BlockSpec((1, 1, d, f), lambda e, i: (l, e, 0, 0)),
                  pl.BlockSpec((1, 1, d, f), lambda e, i: (l, e, 0, 0)),
                  pl.BlockSpec((1, 1, f, d), lambda e, i: (l, e, 0, 0))],
        out_specs=pl.BlockSpec((1, cap, d), lambda e, i: (i, e, 0)),
        out_shape=jax.ShapeDtypeStruct((b, ne * cap, d), BF16),
        scratch_shapes=[pltpu.VMEM((cap, d), F32)],
        compiler_params=_cparams("parallel", "parallel"),
        name="expert_ffn",
    )(pos_t, aff_t, h2, w_gate, w_up, w_down)


def _scatter_kernel(pos_ref, ye_ref, x1_ref, mod_ref, lng_ref, lnb_ref, o_ref, *, cap, ne, alpha):
    ts = pos_ref.shape[1]
    slot = lax.broadcasted_iota(jnp.int32, (ts, cap), 1)
    pos = pos_ref[0]
    onehot = jnp.concatenate(
        [jnp.where(pos[:, e:e + 1] == slot, 1.0, 0.0).astype(BF16) for e in range(ne)], axis=1)
    y = jnp.dot(onehot, ye_ref[0], preferred_element_type=F32)
    y = alpha * x1_ref[0] + (1.0 + mod_ref[0, 0, 5:6, :]) * y
    o_ref[0] = _ln(y) * lng_ref[...] + lnb_ref[...]


def _scatter(pos, ye, x1, mod, ln_g, ln_b, alpha, ne, l):
    b, s, d = x1.shape
    cap = ye.shape[1] // ne
    ts = 512
    return pl.pallas_call(
        functools.partial(_scatter_kernel, cap=cap, ne=ne, alpha=alpha),
        grid=(b, s // ts),
        in_specs=[pl.BlockSpec((1, ts, LANES), lambda i, j: (i, j, 0)),
                  pl.BlockSpec((1, ne * cap, d), lambda i, j: (i, 0, 0)),
                  pl.BlockSpec((1, ts, d), lambda i, j: (i, j, 0)),
                  pl.BlockSpec((1, 1, 6, d), lambda i, j: (l, i, 0, 0)),
                  pl.BlockSpec((1, d), lambda i, j: (0, 0)),
                  pl.BlockSpec((1, d), lambda i, j: (0, 0))],
        out_specs=pl.BlockSpec((1, ts, d), lambda i, j: (i, j, 0)),
        out_shape=jax.ShapeDtypeStruct((b, s, d), F32),
        compiler_params=_cparams("parallel", "parallel"),
        name="scatter_ln",
    )(pos, ye, x1, mod, ln_g, ln_b)


def kernel(x, c, w_ada, b_ada, w_in, conv_w, conv_b, gate_b, sink, rel_bias, w_s, b_s, w_out, w_router,
           w_gate, w_up, w_down, ln_g, ln_b):
    depth = w_ada.shape[0]
    b, s, d = x.shape
    ne = w_router.shape[-1]
    cap = max(1, min(s, EC_CAPACITY_FACTOR * s // ne))
    alpha = float((2 * depth) ** 0.25)

    mod = _modulation(c, w_ada, b_ada).reshape(depth, b, 6, d)
    bias = _band_bias(rel_bias)
    w_in_r = _rearrange_w_in(w_in)
    w_router3 = _split_router(w_router)
    w_out_h, w_gate_h, w_up_h, w_down_h = (w.astype(BF16) for w in (w_out, w_gate, w_up, w_down))

    for l in range(depth):
        att, mqk, mv, mo, gg, mg = _projection(x, mod, w_in_r, l)
        att_o = _attention(att, sink[l], bias)
        ml_o = _mlstm(mqk, mv, mo, mg, gate_b[l], conv_w[l], conv_b[l])
        gm_o = _gmlp(gg, w_s[l], b_s[l])
        x1, h2, logits = _mixout(att_o, ml_o, gm_o, x, mod, w_out_h, ln_g[l, 0:1], ln_b[l, 0:1],
                                 w_router3, alpha, l)
        pos, aff = _route(logits, ne, cap)
        pos_t = pos[:, :, :ne].transpose(0, 2, 1).reshape(b, ne, 1, s)
        aff_t = aff[:, :, :ne].transpose(0, 2, 1).reshape(b, ne, 1, s)
        ye = _expert_ffn(pos_t, aff_t, h2, w_gate_h, w_up_h, w_down_h, cap, l)
        x = _scatter(pos, ye, x1, mod, ln_g[l, 1:2], ln_b[l, 1:2], alpha, ne, l)
    return x
```

```python
import functools

import numpy as np
import jax
import jax.numpy as jnp
from jax import lax
from jax.experimental import pallas as pl
from jax.experimental.pallas import tpu as pltpu

F32 = jnp.float32
BF16 = jnp.bfloat16
HIGHEST = lax.Precision.HIGHEST

HEAD_DIM = 64
LANES = 128
CHUNK = 128
N_ATT_HEADS = 8
N_KV_HEADS = 2
N_ML_HEADS = 4
N_GM_GROUPS = 4
ML_CONV = 3
N_REL_BUCKETS = 32
REL_MAX_DIST = 128
WINDOW = 128
EC_CAPACITY_FACTOR = 2
LN_EPS = 1e-5
NEG_INF = -1e30
VMEM_LIMIT = 60000 * 1024

NT_DIMS = (((1,), (1,)), ((), ()))
TN_DIMS = (((0,), (0,)), ((), ()))


def _cparams(*sem):
    return pltpu.CompilerParams(dimension_semantics=sem, vmem_limit_bytes=VMEM_LIMIT)


def _ln(x):
    mu = jnp.mean(x, axis=-1, keepdims=True)
    xc = x - mu
    var = jnp.mean(xc * xc, axis=-1, keepdims=True)
    return xc * lax.rsqrt(var + LN_EPS)


def _gelu(x):
    return 0.5 * x * (1.0 + lax.erf(x * (2.0 ** -0.5)))


def _silu(x):
    return x * jax.nn.sigmoid(x)


def _mod_kernel(c_ref, w_ref, b_ref, o_ref):
    sc = _silu(c_ref[...])
    o_ref[0] = jnp.dot(sc, w_ref[0], precision=HIGHEST, preferred_element_type=F32) + b_ref[0]


def _modulation(c, w_ada, b_ada):
    depth, d, n = w_ada.shape
    b = c.shape[0]
    tn = 1536
    return pl.pallas_call(
        _mod_kernel,
        grid=(depth, n // tn),
        in_specs=[pl.BlockSpec((b, d), lambda l, j: (0, 0)),
                  pl.BlockSpec((1, d, tn), lambda l, j: (l, 0, j)),
                  pl.BlockSpec((1, 1, tn), lambda l, j: (l, 0, j))],
        out_specs=pl.BlockSpec((1, b, tn), lambda l, j: (l, 0, j)),
        out_shape=jax.ShapeDtypeStruct((depth, b, n), F32),
        compiler_params=_cparams("parallel", "parallel"),
        name="modulation",
    )(c, w_ada, b_ada.reshape(depth, 1, n))


def _t5_buckets(rel):
    nb = N_REL_BUCKETS // 2
    max_exact = nb // 2
    ret = np.where(rel > 0, nb, 0)
    n = np.abs(rel)
    large = max_exact + (np.log(np.maximum(n, 1) / max_exact) / np.log(REL_MAX_DIST / max_exact)
                         * (nb - max_exact)).astype(np.int32)
    large = np.minimum(large, nb - 1)
    return (ret + np.where(n < max_exact, n, large)).astype(np.int32)


def _bias_kernel(rb_ref, bk_ref, mk_ref, o_ref):
    bk = bk_ref[...]
    inband = mk_ref[...] != 0
    for h in range(N_ATT_HEADS):
        acc = jnp.zeros(bk.shape, F32)
        for k in range(N_REL_BUCKETS):
            acc = jnp.where(bk == k, rb_ref[k, h], acc)
        o_ref[h] = jnp.where(inband, acc, NEG_INF)


def _band_bias(rel_bias):
    qi = np.arange(CHUNK)[:, None]
    sj = np.arange(3 * CHUNK)[None, :]
    rel = sj - CHUNK - qi
    buckets = jnp.asarray(_t5_buckets(rel), jnp.int32)
    inband = jnp.asarray((np.abs(rel) <= WINDOW).astype(np.int32))
    return pl.pallas_call(
        _bias_kernel,
        in_specs=[pl.BlockSpec(memory_space=pltpu.SMEM),
                  pl.BlockSpec((CHUNK, 3 * CHUNK), lambda: (0, 0)),
                  pl.BlockSpec((CHUNK, 3 * CHUNK), lambda: (0, 0))],
        out_specs=pl.BlockSpec((N_ATT_HEADS, CHUNK, 3 * CHUNK), lambda: (0, 0, 0)),
        out_shape=jax.ShapeDtypeStruct((N_ATT_HEADS, CHUNK, 3 * CHUNK), F32),
        name="band_bias",
    )(rel_bias, buckets, inband)


ATT_COLS = 1024
MQK_COLS = 512
MV_COLS = 256
MO_COLS = 256
GG_COLS = 512
MG_COLS = 128
PROJ_PIECES = (ATT_COLS, MQK_COLS, MV_COLS, MO_COLS, GG_COLS, MG_COLS)
PROJ_DTYPES = (BF16, F32, BF16, F32, F32, F32)


def _rearrange_w_in(w_in):
    q = w_in[..., 0:512]
    k0, k1 = w_in[..., 512:576], w_in[..., 576:640]
    v0, v1 = w_in[..., 640:704], w_in[..., 704:768]
    rest = w_in[..., 768:1792]
    mg = w_in[..., 1792:1808]
    gg = w_in[..., 1808:2320]
    pad = jnp.zeros(w_in.shape[:-1] + (MG_COLS - mg.shape[-1],), w_in.dtype)
    return jnp.concatenate([q, k0, k0, k1, k1, v0, v0, v1, v1, rest, gg, mg, pad], axis=-1).astype(BF16)


def _proj_kernel(x_ref, mod_ref, w_ref, *rest):
    out_refs, h_scr = rest[:-1], rest[-1]
    h = _ln(x_ref[0]) * (1.0 + mod_ref[0, 0, 1:2, :]) + mod_ref[0, 0, 0:1, :]
    h_scr[...] = h.astype(BF16)
    off = 0
    for width, o_ref in zip(PROJ_PIECES, out_refs):
        o_ref[0] = jnp.dot(h_scr[...], w_ref[0, :, off:off + width],
                           preferred_element_type=F32).astype(o_ref.dtype)
        off += width


def _projection(x, mod, w_in_r, l):
    b, s, d = x.shape
    tm = 512
    n = w_in_r.shape[-1]
    return pl.pallas_call(
        _proj_kernel,
        grid=(b, s // tm),
        in_specs=[pl.BlockSpec((1, tm, d), lambda i, j: (i, j, 0)),
                  pl.BlockSpec((1, 1, 6, d), lambda i, j: (l, i, 0, 0)),
                  pl.BlockSpec((1, d, n), lambda i, j: (l, 0, 0))],
        out_specs=[pl.BlockSpec((1, tm, w), lambda i, j: (i, j, 0)) for w in PROJ_PIECES],
        out_shape=[jax.ShapeDtypeStruct((b, s, w), dt) for w, dt in zip(PROJ_PIECES, PROJ_DTYPES)],
        scratch_shapes=[pltpu.VMEM((tm, d), BF16)],
        compiler_params=_cparams("parallel", "parallel"),
        name="proj",
    )(x, mod, w_in_r)


def _attn_kernel(sink_ref, q_ref, kv_ref, bias_ref, o_ref, *, nb, tq):
    t = pl.program_id(1)
    lane = lax.broadcasted_iota(jnp.int32, (1, LANES), 1)
    lo = lane < HEAD_DIM
    col = lax.broadcasted_iota(jnp.int32, (1, 3 * CHUNK), 1)
    scale = HEAD_DIM ** -0.5
    for j in range(tq // CHUNK):
        n = t * (tq // CHUNK) + j
        ps = pl.multiple_of(jnp.maximum(n - 1, 0) * CHUNK, CHUNK)
        cs = pl.multiple_of(n * CHUNK, CHUNK)
        ns = pl.multiple_of(jnp.minimum(n + 1, nb - 1) * CHUNK, CHUNK)
        kvb = jnp.concatenate([kv_ref[0, pl.ds(ps, CHUNK), :],
                               kv_ref[0, pl.ds(cs, CHUNK), :],
                               kv_ref[0, pl.ds(ns, CHUNK), :]], axis=0)
        valid = jnp.logical_and(jnp.logical_or(col >= CHUNK, n > 0),
                                jnp.logical_or(col < 2 * CHUNK, n < nb - 1))
        q = q_ref[0, j * CHUNK:(j + 1) * CHUNK, :]
        outs = []
        for grp in range(N_ATT_HEADS // 2):
            kvh = grp // 2
            kd = kvb[:, kvh * LANES:(kvh + 1) * LANES]
            vd = kvb[:, (2 + kvh) * LANES:(3 + kvh) * LANES]
            qg = q[:, grp * LANES:(grp + 1) * LANES]
            zq = jnp.zeros_like(qg)
            q2 = jnp.concatenate([jnp.where(lo, qg, zq), jnp.where(lo, zq, qg)], axis=0)
            logits = lax.dot_general(q2, kd, NT_DIMS, preferred_element_type=F32) * scale
            acc = None
            for g in range(2):
                h = grp * 2 + g
                l = logits[g * CHUNK:(g + 1) * CHUNK] + bias_ref[h]
                l = jnp.where(valid, l, NEG_INF)
                sk = sink_ref[h]
                m = jnp.maximum(jnp.max(l, axis=1, keepdims=True), sk)
                p = jnp.exp(l - m)
                den = jnp.sum(p, axis=1, keepdims=True) + jnp.exp(sk - m)
                zv = jnp.zeros_like(vd)
                vh = jnp.where(lo, vd, zv) if g == 0 else jnp.where(lo, zv, vd)
                part = jnp.dot(p.astype(BF16), vh, preferred_element_type=F32) / den
                acc = part if acc is None else acc + part
            outs.append(acc)
        o_ref[0, j * CHUNK:(j + 1) * CHUNK, :] = jnp.concatenate(outs, axis=1).astype(o_ref.dtype)


def _attention(att, sink, bias):
    b, s, _ = att.shape
    tq = 512
    nb = s // CHUNK
    return pl.pallas_call(
        functools.partial(_attn_kernel, nb=nb, tq=tq),
        grid=(b, s // tq),
        in_specs=[pl.BlockSpec(memory_space=pltpu.SMEM),
                  pl.BlockSpec((1, tq, 512), lambda i, j: (i, j, 0)),
                  pl.BlockSpec((1, s, 512), lambda i, j: (i, 0, 1)),
                  pl.BlockSpec((N_ATT_HEADS, CHUNK, 3 * CHUNK), lambda i, j: (0, 0, 0))],
        out_specs=pl.BlockSpec((1, tq, 512), lambda i, j: (i, j, 0)),
        out_shape=jax.ShapeDtypeStruct((b, s, 512), BF16),
        compiler_params=_cparams("parallel", "parallel"),
        name="attn",
    )(sink, att, att, bias)


def _mlstm_kernel(mqk_ref, mv_ref, mo_ref, g4_ref, gb_ref, cw_ref, cb_ref, o_ref,
                  qk_scr, rrow_scr, rcol_scr, hf_scr, hb_scr, st_scr, m_scr, *, nc):
    L = CHUNK
    s_len = nc * L
    ng = 4 * N_ML_HEADS
    lane = lax.broadcasted_iota(jnp.int32, (1, LANES), 1)
    lo = lane < HEAD_DIM

    w0, w1, w2, cb = cw_ref[0:1, :], cw_ref[1:2, :], cw_ref[2:3, :], cb_ref[...]
    row = lax.broadcasted_iota(jnp.int32, (L, 1), 0)
    lane_qk = lax.broadcasted_iota(jnp.int32, (1, 4 * LANES), 1)
    kscale = jnp.where(lane_qk >= 2 * LANES, HEAD_DIM ** -0.5, 1.0).astype(F32)

    def conv_body(c, carry):
        r0 = pl.multiple_of(c * L, L)
        xc = mqk_ref[0, pl.ds(r0, L), :]
        pr = mqk_ref[0, pl.ds(jnp.maximum(r0 - 1, 0), 1), :]
        nx = mqk_ref[0, pl.ds(jnp.minimum(r0 + L, s_len - 1), 1), :]
        pr = jnp.where(c > 0, pr, 0.0)
        nx = jnp.where(c < nc - 1, nx, 0.0)
        xp = jnp.where(row == 0, pr, pltpu.roll(xc, 1, 0))
        xn = jnp.where(row == L - 1, nx, pltpu.roll(xc, L - 1, 0))
        y = w0 * xp + w1 * xc + w2 * xn + cb
        qk_scr[pl.ds(r0, L), :] = (_silu(y) * kscale).astype(BF16)
        return carry

    lax.fori_loop(0, nc, conv_body, 0)

    g = (g4_ref[0] + gb_ref[...]).reshape(nc * ng, L)
    kind = (lax.broadcasted_iota(jnp.int32, (nc * ng, 1), 0) % ng) // N_ML_HEADS
    logf = jnp.minimum(g, 0.0) - jnp.log1p(jnp.exp(-jnp.abs(g)))
    x = jnp.where(jnp.logical_or(kind == 1, kind == 3), logf, 0.0)
    iu = lax.broadcasted_iota(jnp.int32, (L, L), 0)
    it = lax.broadcasted_iota(jnp.int32, (L, L), 1)
    prefix = jnp.dot(x, (iu <= it).astype(F32), precision=HIGHEST, preferred_element_type=F32)
    suffix = jnp.dot(x, (iu >= it).astype(F32), precision=HIGHEST, preferred_element_type=F32)
    r = jnp.where(kind == 1, prefix, jnp.where(kind == 3, suffix, g))
    rrow_scr[...] = r.reshape(nc, ng, L)
    zpad = jnp.zeros((L - ng, L), F32)
    for c in range(nc):
        rcol_scr[c] = jnp.concatenate([r[c * ng:(c + 1) * ng], zpad], axis=0).T

    st_scr[...] = jnp.zeros(st_scr.shape, F32)
    m_scr[...] = jnp.zeros(m_scr.shape, F32)

    ones_blk = jnp.ones((L, LANES), BF16)

    def chunk_dir(c, dirn, h_scr):
        r0 = pl.multiple_of(c * L, L)
        qk = qk_scr[pl.ds(r0, L), :]
        vrows = mv_ref[0, pl.ds(r0, L), :]
        rr = rrow_scr[c]
        rc = rcol_scr[c]
        causal = (it <= iu) if dirn == 0 else (it >= iu)
        for grp in range(N_ML_HEADS // 2):
            qg = qk[:, grp * LANES:(grp + 1) * LANES]
            kg = qk[:, (2 + grp) * LANES:(3 + grp) * LANES]
            vg = vrows[:, grp * LANES:(grp + 1) * LANES]
            zb = jnp.zeros_like(qg)
            hv = []
            for odd in range(2):
                h = grp * 2 + odd
                hm = lo if odd == 0 else jnp.logical_not(lo)
                qh = jnp.where(hm, qg, zb)
                kh = jnp.where(hm, kg, zb)
                vext = jnp.concatenate([jnp.where(hm, vg, zb), ones_blk], axis=1)
                ii, bi = dirn * 2 * N_ML_HEADS + h, dirn * 2 * N_ML_HEADS + N_ML_HEADS + h
                i_row, b_row = rr[ii:ii + 1, :], rr[bi:bi + 1, :]
                i_col, b_col = rc[:, ii:ii + 1], rc[:, bi:bi + 1]
                b_last = b_row[:, L - 1:L] if dirn == 0 else b_row[:, 0:1]
                sidx = dirn * N_ML_HEADS + h
                m_old = m_scr[sidx][:, 0:1]
                st = st_scr[sidx]
                a_row = i_row - b_row
                a_max = jnp.max(jnp.where(causal, a_row, -jnp.inf), axis=1, keepdims=True)
                m_t = jnp.maximum(b_col + m_old, b_col + a_max)
                u = jnp.broadcast_to(b_col - m_t, (L, L))
                smat = (lax.dot_general(qh, kg, NT_DIMS, preferred_element_type=F32)
                        * jnp.exp(jnp.where(causal, u + a_row, -jnp.inf)))
                iw = jnp.concatenate([jnp.exp(u + m_old)] * 2, axis=1)
                num = (jnp.dot(smat.astype(BF16), vext, preferred_element_type=F32)
                       + iw * jnp.dot(qh, st.astype(BF16), preferred_element_type=F32))
                hv.append(num[:, :LANES] / jnp.maximum(jnp.abs(num[:, LANES:]), jnp.exp(-m_t)))
                ws_col = b_last - b_col + i_col
                ws_row = b_last - b_row + i_row
                m_new = jnp.maximum(b_last + m_old, jnp.max(ws_row, axis=1, keepdims=True))
                wv = (jnp.exp(ws_col - m_new) * vext.astype(F32)).astype(BF16)
                decay = jnp.exp(b_last + m_old - m_new)
                st_scr[sidx] = decay * st + lax.dot_general(kh, wv, TN_DIMS, preferred_element_type=F32)
                m_scr[sidx] = jnp.broadcast_to(m_new, (1, LANES))
            h_scr[pl.ds(r0, L), grp * LANES:(grp + 1) * LANES] = jnp.where(lo, hv[0], hv[1])

    def body(i, carry):
        chunk_dir(i, 0, hf_scr)
        chunk_dir(nc - 1 - i, 1, hb_scr)
        return carry

    lax.fori_loop(0, nc, body, 0)
    o_ref[0] = (jax.nn.sigmoid(mo_ref[0]) * (hf_scr[...] + hb_scr[...])).astype(o_ref.dtype)


def _mlstm(mqk, mv, mo, mg, gate_b, conv_w, conv_b):
    b, s, _ = mqk.shape
    nc = s // CHUNK
    ng = 4 * N_ML_HEADS
    g4 = mg[:, :, :ng].reshape(b, nc, CHUNK, ng).transpose(0, 1, 3, 2)
    seq = lambda w: pl.BlockSpec((1, s, w), lambda i: (i, 0, 0))
    return pl.pallas_call(
        functools.partial(_mlstm_kernel, nc=nc),
        grid=(b,),
        in_specs=[seq(512), seq(256), seq(256),
                  pl.BlockSpec((1, nc, ng, CHUNK), lambda i: (i, 0, 0, 0)),
                  pl.BlockSpec((ng, 1), lambda i: (0, 0)),
                  pl.BlockSpec((ML_CONV, 512), lambda i: (0, 0)),
                  pl.BlockSpec((1, 512), lambda i: (0, 0))],
        out_specs=seq(256),
        out_shape=jax.ShapeDtypeStruct((b, s, 256), BF16),
        scratch_shapes=[pltpu.VMEM((s, 512), BF16),
                        pltpu.VMEM((nc, ng, CHUNK), F32),
                        pltpu.VMEM((nc, CHUNK, LANES), F32),
                        pltpu.VMEM((s, 256), F32),
                        pltpu.VMEM((s, 256), F32),
                        pltpu.VMEM((2 * N_ML_HEADS, LANES, 2 * LANES), F32),
                        pltpu.VMEM((2 * N_ML_HEADS, 1, LANES), F32)],
        compiler_params=_cparams("parallel"),
        name="mlstm",
    )(mqk, mv, mo, g4, gate_b.reshape(ng, 1), conv_w, conv_b.reshape(1, 512))


def _gmlp_kernel(gg_ref, ws_ref, bs_ref, o_ref, *, tg):
    lane = lax.broadcasted_iota(jnp.int32, (1, 2 * LANES), 1)
    for j in range(tg // CHUNK):
        rows = slice(j * CHUNK, (j + 1) * CHUNK)
        u = _gelu(gg_ref[0, rows, 0:256])
        v = _ln(_gelu(gg_ref[0, rows, 256:512])).astype(BF16)
        zv = jnp.zeros_like(v)
        s = None
        for g in range(N_GM_GROUPS):
            ing = jnp.logical_and(lane >= g * HEAD_DIM, lane < (g + 1) * HEAD_DIM)
            part = jnp.dot(ws_ref[g], jnp.where(ing, v, zv), preferred_element_type=F32)
            part = part + jnp.where(ing, bs_ref[:, g:g + 1], 0.0)
            s = part if s is None else s + part
        o_ref[0, rows, :] = (u * s).astype(o_ref.dtype)


def _gmlp(gg, w_s, b_s):
    b, s, _ = gg.shape
    tg = 512
    return pl.pallas_call(
        functools.partial(_gmlp_kernel, tg=tg),
        grid=(b, s // tg),
        in_specs=[pl.BlockSpec((1, tg, 512), lambda i, j: (i, j, 0)),
                  pl.BlockSpec((N_GM_GROUPS, CHUNK, CHUNK), lambda i, j: (0, 0, 0)),
                  pl.BlockSpec((CHUNK, N_GM_GROUPS), lambda i, j: (0, 0))],
        out_specs=pl.BlockSpec((1, tg, 256), lambda i, j: (i, j, 0)),
        out_shape=jax.ShapeDtypeStruct((b, s, 256), BF16),
        compiler_params=_cparams("parallel", "parallel"),
        name="gmlp",
    )(gg, w_s.astype(BF16), b_s.T)


def _mixout_kernel(att_ref, ml_ref, gm_ref, x_ref, mod_ref, wo_ref, lng_ref, lnb_ref, wr_ref,
                   x1_ref, h2_ref, lg_ref, *, alpha):
    mix = jnp.dot(att_ref[0], wo_ref[0, 0:512, :], preferred_element_type=F32)
    mix = mix + jnp.dot(ml_ref[0], wo_ref[0, 512:768, :], preferred_element_type=F32)
    mix = mix + jnp.dot(gm_ref[0], wo_ref[0, 768:1024, :], preferred_element_type=F32)
    y = alpha * x_ref[0] + (1.0 + mod_ref[0, 0, 2:3, :]) * mix
    x1 = _ln(y) * lng_ref[...] + lnb_ref[...]
    x1_ref[0] = x1
    h2 = _ln(x1) * (1.0 + mod_ref[0, 0, 4:5, :]) + mod_ref[0, 0, 3:4, :]
    hi = h2.astype(BF16)
    h2_ref[0] = hi
    lo = (h2 - hi.astype(F32)).astype(BF16)
    lg_ref[0] = jnp.dot(jnp.concatenate([hi, lo, hi], axis=1), wr_ref[0], preferred_element_type=F32)


def _split_router(w_router):
    w = jnp.pad(w_router, ((0, 0), (0, 0), (0, LANES - w_router.shape[-1])))
    hi = w.astype(BF16)
    lo = (w - hi.astype(F32)).astype(BF16)
    return jnp.concatenate([hi, hi, lo], axis=1)


def _mixout(att, ml, gm, x, mod, w_out, ln_g, ln_b, w_router3, alpha, l):
    b, s, d = x.shape
    tm = 512
    row = lambda w: pl.BlockSpec((1, tm, w), lambda i, j: (i, j, 0))
    full = lambda a: pl.BlockSpec(a.shape, lambda i, j: (0,) * a.ndim)
    layer = lambda a: pl.BlockSpec((1,) + a.shape[1:], lambda i, j: (l,) + (0,) * (a.ndim - 1))
    return pl.pallas_call(
        functools.partial(_mixout_kernel, alpha=alpha),
        grid=(b, s // tm),
        in_specs=[row(512), row(256), row(256), row(d),
                  pl.BlockSpec((1, 1, 6, d), lambda i, j: (l, i, 0, 0)),
                  layer(w_out), full(ln_g), full(ln_b), layer(w_router3)],
        out_specs=[row(d), row(d), row(LANES)],
        out_shape=[jax.ShapeDtypeStruct((b, s, d), F32),
                   jax.ShapeDtypeStruct((b, s, d), BF16),
                   jax.ShapeDtypeStruct((b, s, LANES), F32)],
        compiler_params=_cparams("parallel", "parallel"),
        name="mixout",
    )(att, ml, gm, x, mod, w_out, ln_g, ln_b, w_router3)


def _route_kernel(lg_ref, pos_ref, aff_ref, *, cap, ne, nc):
    L = CHUNK
    lane = lax.broadcasted_iota(jnp.int32, (1, LANES), 1)
    l = jnp.where(lane < ne, lg_ref[0], -jnp.inf)
    e = jnp.exp(l - jnp.max(l, axis=1, keepdims=True))
    aff = e / jnp.sum(e, axis=1, keepdims=True)
    aff_ref[0] = aff
    bits = pltpu.bitcast(aff, jnp.int32)

    def bit_body(k, thr):
        cand = jnp.bitwise_or(thr, jnp.left_shift(jnp.int32(1), 30 - k))
        cnt = jnp.sum((bits >= cand).astype(jnp.int32), axis=0, keepdims=True)
        return jnp.where(cnt >= cap, cand, thr)

    thr = lax.fori_loop(0, 31, bit_body, jnp.zeros((1, LANES), jnp.int32))
    n_gt = jnp.sum((bits > thr).astype(jnp.int32), axis=0, keepdims=True)
    need = (cap - n_gt).astype(F32)

    it = lax.broadcasted_iota(jnp.int32, (L, L), 0)
    iu = lax.broadcasted_iota(jnp.int32, (L, L), 1)
    before = (iu < it).astype(BF16)

    run_eq = run_sel = jnp.zeros((1, LANES), F32)
    for c in range(nc):
        rows = slice(c * L, (c + 1) * L)
        bc = pltpu.bitcast(aff_ref[0, rows, :], jnp.int32)
        eq = bc == thr
        eqf = jnp.where(eq, 1.0, 0.0)
        eq_before = jnp.dot(before, eqf.astype(BF16), preferred_element_type=F32) + run_eq
        sel = jnp.logical_or(bc > thr, jnp.logical_and(eq, eq_before < need))
        self_ = jnp.where(sel, 1.0, 0.0)
        sel_before = jnp.dot(before, self_.astype(BF16), preferred_element_type=F32) + run_sel
        pos_ref[0, rows, :] = jnp.where(sel, sel_before.astype(jnp.int32), -1)
        run_eq = run_eq + jnp.sum(eqf, axis=0, keepdims=True)
        run_sel = run_sel + jnp.sum(self_, axis=0, keepdims=True)


def _route(logits, ne, cap):
    b, s, _ = logits.shape
    blk = pl.BlockSpec((1, s, LANES), lambda i: (i, 0, 0))
    return pl.pallas_call(
        functools.partial(_route_kernel, cap=cap, ne=ne, nc=s // CHUNK),
        grid=(b,),
        in_specs=[blk],
        out_specs=[blk, blk],
        out_shape=[jax.ShapeDtypeStruct((b, s, LANES), jnp.int32),
                   jax.ShapeDtypeStruct((b, s, LANES), F32)],
        compiler_params=_cparams("parallel"),
        name="route",
    )(logits)


def _ffn_kernel(pos_ref, aff_ref, h2_ref, wg_ref, wu_ref, wd_ref, o_ref, acc_scr, *, cap, tf):
    nseq, s = h2_ref.shape[0], h2_ref.shape[1]
    slot = lax.broadcasted_iota(jnp.int32, (cap, s), 0)
    xgs, gates = [], []
    for j in range(nseq):
        hit = pos_ref[j, 0] == slot
        onehot = jnp.where(hit, 1.0, 0.0).astype(BF16)
        gates.append(jnp.sum(jnp.where(hit, aff_ref[j, 0], 0.0), axis=1, keepdims=True))
        xgs.append(jnp.dot(onehot, h2_ref[j], preferred_element_type=F32).astype(BF16))
    xg = jnp.concatenate(xgs, axis=0)
    f = wg_ref.shape[3]
    for k in range(f // tf):
        cols = slice(k * tf, (k + 1) * tf)
        a = jnp.dot(xg, wg_ref[0, 0, :, cols], preferred_element_type=F32)
        u = jnp.dot(xg, wu_ref[0, 0, :, cols], preferred_element_type=F32)
        hid = (_silu(a) * u).astype(BF16)
        part = jnp.dot(hid, wd_ref[0, 0, cols, :], preferred_element_type=F32)
        if k == 0:
            acc_scr[...] = part
        else:
            acc_scr[...] += part
    for j in range(nseq):
        o_ref[j] = (acc_scr[j * cap:(j + 1) * cap, :] * gates[j]).astype(o_ref.dtype)


FFN_SEQS = 2


def _expert_ffn(pos_t, aff_t, h2, w_gate, w_up, w_down, cap, l):
    b, s, d = h2.shape
    _, ne, _, f = w_gate.shape
    nseq = FFN_SEQS if b % FFN_SEQS == 0 else 1
    return pl.pallas_call(
        functools.partial(_ffn_kernel, cap=cap, tf=512),
        grid=(ne, b // nseq),
        in_specs=[pl.BlockSpec((nseq, 1, 1, s), lambda e, i: (i, e, 0, 0)),
                  pl.BlockSpec((nseq, 1, 1, s), lambda e, i: (i, e, 0, 0)),
                  pl.BlockSpec((nseq, s, d), lambda e, i: (i, 0, 0)),
                  pl.BlockSpec((1, 1, d, f), lambda e, i: (l, e, 0, 0)),
                  pl.BlockSpec((1, 1, d, f), lambda e, i: (l, e, 0, 0)),
                  pl.BlockSpec((1, 1, f, d), lambda e, i: (l, e, 0, 0))],
        out_specs=pl.BlockSpec((nseq, cap, d), lambda e, i: (i, e, 0)),
        out_shape=jax.ShapeDtypeStruct((b, ne * cap, d), BF16),
        scratch_shapes=[pltpu.VMEM((nseq * cap, d), F32)],
        compiler_params=_cparams("parallel", "parallel"),
        name="expert_ffn",
    )(pos_t, aff_t, h2, w_gate, w_up, w_down)


def _scatter_kernel(pos_ref, ye_ref, x1_ref, mod_ref, lng_ref, lnb_ref, o_ref, *, cap, ne, alpha):
    ts = pos_ref.shape[1]
    slot = lax.broadcasted_iota(jnp.int32, (ts, cap), 1)
    pos = pos_ref[0]
    onehot = jnp.concatenate(
        [jnp.where(pos[:, e:e + 1] == slot, 1.0, 0.0).astype(BF16) for e in range(ne)], axis=1)
    y = jnp.dot(onehot, ye_ref[0], preferred_element_type=F32)
    y = alpha * x1_ref[0] + (1.0 + mod_ref[0, 0, 5:6, :]) * y
    o_ref[0] = _ln(y) * lng_ref[...] + lnb_ref[...]


def _scatter(pos, ye, x1, mod, ln_g, ln_b, alpha, ne, l):
    b, s, d = x1.shape
    cap = ye.shape[1] // ne
    ts = 512
    return pl.pallas_call(
        functools.partial(_scatter_kernel, cap=cap, ne=ne, alpha=alpha),
        grid=(b, s // ts),
        in_specs=[pl.BlockSpec((1, ts, LANES), lambda i, j: (i, j, 0)),
                  pl.BlockSpec((1, ne * cap, d), lambda i, j: (i, 0, 0)),
                  pl.BlockSpec((1, ts, d), lambda i, j: (i, j, 0)),
                  pl.BlockSpec((1, 1, 6, d), lambda i, j: (l, i, 0, 0)),
                  pl.BlockSpec((1, d), lambda i, j: (0, 0)),
                  pl.BlockSpec((1, d), lambda i, j: (0, 0))],
        out_specs=pl.BlockSpec((1, ts, d), lambda i, j: (i, j, 0)),
        out_shape=jax.ShapeDtypeStruct((b, s, d), F32),
        compiler_params=_cparams("parallel", "parallel"),
        name="scatter_ln",
    )(pos, ye, x1, mod, ln_g, ln_b)


def kernel(x, c, w_ada, b_ada, w_in, conv_w, conv_b, gate_b, sink, rel_bias, w_s, b_s, w_out, w_router,
           w_gate, w_up, w_down, ln_g, ln_b):
    depth = w_ada.shape[0]
    b, s, d = x.shape
    ne = w_router.shape[-1]
    cap = max(1, min(s, EC_CAPACITY_FACTOR * s // ne))
    alpha = float((2 * depth) ** 0.25)

    mod = _modulation(c, w_ada, b_ada).reshape(depth, b, 6, d)
    bias = _band_bias(rel_bias)
    w_in_r = _rearrange_w_in(w_in)
    w_router3 = _split_router(w_router)
    w_out_h, w_gate_h, w_up_h, w_down_h = (w.astype(BF16) for w in (w_out, w_gate, w_up, w_down))

    for l in range(depth):
        att, mqk, mv, mo, gg, mg = _projection(x, mod, w_in_r, l)
        att_o = _attention(att, sink[l], bias)
        ml_o = _mlstm(mqk, mv, mo, mg, gate_b[l], conv_w[l], conv_b[l])
        gm_o = _gmlp(gg, w_s[l], b_s[l])
        x1, h2, logits = _mixout(att_o, ml_o, gm_o, x, mod, w_out_h, ln_g[l, 0:1], ln_b[l, 0:1],
                                 w_router3, alpha, l)
        pos, aff = _route(logits, ne, cap)
        pos_t = pos[:, :, :ne].transpose(0, 2, 1).reshape(b, ne, 1, s)
        aff_t = aff[:, :, :ne].transpose(0, 2, 1).reshape(b, ne, 1, s)
        ye = _expert_ffn(pos_t, aff_t, h2, w_gate_h, w_up_h, w_down_h, cap, l)
        x = _scatter(pos, ye, x1, mod, ln_g[l, 1:2], ln_b[l, 1:2], alpha, ne, l)
    return x
```

```python
import functools

import numpy as np
import jax
import jax.numpy as jnp
from jax import lax
from jax.experimental import pallas as pl
from jax.experimental.pallas import tpu as pltpu

F32 = jnp.float32
BF16 = jnp.bfloat16
HIGHEST = lax.Precision.HIGHEST

HEAD_DIM = 64
LANES = 128
CHUNK = 128
N_ATT_HEADS = 8
N_KV_HEADS = 2
N_ML_HEADS = 4
N_GM_GROUPS = 4
ML_CONV = 3
N_REL_BUCKETS = 32
REL_MAX_DIST = 128
WINDOW = 128
EC_CAPACITY_FACTOR = 2
LN_EPS = 1e-5
NEG_INF = -1e30
VMEM_LIMIT = 60000 * 1024

NT_DIMS = (((1,), (1,)), ((), ()))
TN_DIMS = (((0,), (0,)), ((), ()))


def _cparams(*sem):
    return pltpu.CompilerParams(dimension_semantics=sem, vmem_limit_bytes=VMEM_LIMIT)


def _ln(x):
    mu = jnp.mean(x, axis=-1, keepdims=True)
    xc = x - mu
    var = jnp.mean(xc * xc, axis=-1, keepdims=True)
    return xc * lax.rsqrt(var + LN_EPS)


def _gelu(x):
    return 0.5 * x * (1.0 + lax.erf(x * (2.0 ** -0.5)))


def _silu(x):
    return x * jax.nn.sigmoid(x)


def _mod_kernel(c_ref, w_ref, b_ref, o_ref):
    sc = _silu(c_ref[...])
    o_ref[0] = jnp.dot(sc, w_ref[0], precision=HIGHEST, preferred_element_type=F32) + b_ref[0]


def _modulation(c, w_ada, b_ada):
    depth, d, n = w_ada.shape
    b = c.shape[0]
    tn = 1536
    return pl.pallas_call(
        _mod_kernel,
        grid=(depth, n // tn),
        in_specs=[pl.BlockSpec((b, d), lambda l, j: (0, 0)),
                  pl.BlockSpec((1, d, tn), lambda l, j: (l, 0, j)),
                  pl.BlockSpec((1, 1, tn), lambda l, j: (l, 0, j))],
        out_specs=pl.BlockSpec((1, b, tn), lambda l, j: (l, 0, j)),
        out_shape=jax.ShapeDtypeStruct((depth, b, n), F32),
        compiler_params=_cparams("parallel", "parallel"),
        name="modulation",
    )(c, w_ada, b_ada.reshape(depth, 1, n))


def _t5_buckets(rel):
    nb = N_REL_BUCKETS // 2
    max_exact = nb // 2
    ret = np.where(rel > 0, nb, 0)
    n = np.abs(rel)
    large = max_exact + (np.log(np.maximum(n, 1) / max_exact) / np.log(REL_MAX_DIST / max_exact)
                         * (nb - max_exact)).astype(np.int32)
    large = np.minimum(large, nb - 1)
    return (ret + np.where(n < max_exact, n, large)).astype(np.int32)


def _bias_kernel(rb_ref, bk_ref, mk_ref, o_ref):
    bk = bk_ref[...]
    inband = mk_ref[...] != 0
    for h in range(N_ATT_HEADS):
        acc = jnp.zeros(bk.shape, F32)
        for k in range(N_REL_BUCKETS):
            acc = jnp.where(bk == k, rb_ref[k, h], acc)
        o_ref[h] = jnp.where(inband, acc, NEG_INF)


def _band_bias(rel_bias):
    qi = np.arange(CHUNK)[:, None]
    sj = np.arange(3 * CHUNK)[None, :]
    rel = sj - CHUNK - qi
    buckets = jnp.asarray(_t5_buckets(rel), jnp.int32)
    inband = jnp.asarray((np.abs(rel) <= WINDOW).astype(np.int32))
    return pl.pallas_call(
        _bias_kernel,
        in_specs=[pl.BlockSpec(memory_space=pltpu.SMEM),
                  pl.BlockSpec((CHUNK, 3 * CHUNK), lambda: (0, 0)),
                  pl.BlockSpec((CHUNK, 3 * CHUNK), lambda: (0, 0))],
        out_specs=pl.BlockSpec((N_ATT_HEADS, CHUNK, 3 * CHUNK), lambda: (0, 0, 0)),
        out_shape=jax.ShapeDtypeStruct((N_ATT_HEADS, CHUNK, 3 * CHUNK), F32),
        name="band_bias",
    )(rel_bias, buckets, inband)


ATT_COLS = 1024
MQK_COLS = 512
MV_COLS = 256
MO_COLS = 256
GG_COLS = 512
MG_COLS = 128
PROJ_PIECES = (ATT_COLS, MQK_COLS, MV_COLS, MO_COLS, GG_COLS, MG_COLS)
PROJ_DTYPES = (BF16, F32, BF16, F32, F32, F32)


def _rearrange_w_in(w_in):
    q = w_in[..., 0:512]
    k0, k1 = w_in[..., 512:576], w_in[..., 576:640]
    v0, v1 = w_in[..., 640:704], w_in[..., 704:768]
    rest = w_in[..., 768:1792]
    mg = w_in[..., 1792:1808]
    gg = w_in[..., 1808:2320]
    pad = jnp.zeros(w_in.shape[:-1] + (MG_COLS - mg.shape[-1],), w_in.dtype)
    return jnp.concatenate([q, k0, k0, k1, k1, v0, v0, v1, v1, rest, gg, mg, pad], axis=-1).astype(BF16)


def _proj_kernel(x_ref, mod_ref, w_ref, *rest):
    out_refs, h_scr = rest[:-1], rest[-1]
    h = _ln(x_ref[0]) * (1.0 + mod_ref[0, 0, 1:2, :]) + mod_ref[0, 0, 0:1, :]
    h_scr[...] = h.astype(BF16)
    off = 0
    for width, o_ref in zip(PROJ_PIECES, out_refs):
        o_ref[0] = jnp.dot(h_scr[...], w_ref[0, :, off:off + width],
                           preferred_element_type=F32).astype(o_ref.dtype)
        off += width


def _projection(x, mod, w_in_r, l):
    b, s, d = x.shape
    tm = 512
    n = w_in_r.shape[-1]
    return pl.pallas_call(
        _proj_kernel,
        grid=(b, s // tm),
        in_specs=[pl.BlockSpec((1, tm, d), lambda i, j: (i, j, 0)),
                  pl.BlockSpec((1, 1, 6, d), lambda i, j: (l, i, 0, 0)),
                  pl.BlockSpec((1, d, n), lambda i, j: (l, 0, 0))],
        out_specs=[pl.BlockSpec((1, tm, w), lambda i, j: (i, j, 0)) for w in PROJ_PIECES],
        out_shape=[jax.ShapeDtypeStruct((b, s, w), dt) for w, dt in zip(PROJ_PIECES, PROJ_DTYPES)],
        scratch_shapes=[pltpu.VMEM((tm, d), BF16)],
        compiler_params=_cparams("parallel", "parallel"),
        name="proj",
    )(x, mod, w_in_r)


def _attn_kernel(sink_ref, q_ref, kv_ref, bias_ref, o_ref, *, nb, tq):
    t = pl.program_id(1)
    lane = lax.broadcasted_iota(jnp.int32, (1, LANES), 1)
    lo = lane < HEAD_DIM
    col = lax.broadcasted_iota(jnp.int32, (1, 3 * CHUNK), 1)
    scale = HEAD_DIM ** -0.5
    for j in range(tq // CHUNK):
        n = t * (tq // CHUNK) + j
        ps = pl.multiple_of(jnp.maximum(n - 1, 0) * CHUNK, CHUNK)
        cs = pl.multiple_of(n * CHUNK, CHUNK)
        ns = pl.multiple_of(jnp.minimum(n + 1, nb - 1) * CHUNK, CHUNK)
        kvb = jnp.concatenate([kv_ref[0, pl.ds(ps, CHUNK), :],
                               kv_ref[0, pl.ds(cs, CHUNK), :],
                               kv_ref[0, pl.ds(ns, CHUNK), :]], axis=0)
        valid = jnp.logical_and(jnp.logical_or(col >= CHUNK, n > 0),
                                jnp.logical_or(col < 2 * CHUNK, n < nb - 1))
        q = q_ref[0, j * CHUNK:(j + 1) * CHUNK, :]
        outs = []
        for grp in range(N_ATT_HEADS // 2):
            kvh = grp // 2
            kd = kvb[:, kvh * LANES:(kvh + 1) * LANES]
            vd = kvb[:, (2 + kvh) * LANES:(3 + kvh) * LANES]
            qg = q[:, grp * LANES:(grp + 1) * LANES]
            zq = jnp.zeros_like(qg)
            q2 = jnp.concatenate([jnp.where(lo, qg, zq), jnp.where(lo, zq, qg)], axis=0)
            logits = lax.dot_general(q2, kd, NT_DIMS, preferred_element_type=F32) * scale
            acc = None
            for g in range(2):
                h = grp * 2 + g
                l = logits[g * CHUNK:(g + 1) * CHUNK] + bias_ref[h]
                l = jnp.where(valid, l, NEG_INF)
                sk = sink_ref[h]
                m = jnp.maximum(jnp.max(l, axis=1, keepdims=True), sk)
                p = jnp.exp(l - m)
                den = jnp.sum(p, axis=1, keepdims=True) + jnp.exp(sk - m)
                zv = jnp.zeros_like(vd)
                vh = jnp.where(lo, vd, zv) if g == 0 else jnp.where(lo, zv, vd)
                part = jnp.dot(p.astype(BF16), vh, preferred_element_type=F32) / den
                acc = part if acc is None else acc + part
            outs.append(acc)
        o_ref[0, j * CHUNK:(j + 1) * CHUNK, :] = jnp.concatenate(outs, axis=1).astype(o_ref.dtype)


def _attention(att, sink, bias):
    b, s, _ = att.shape
    tq = 512
    nb = s // CHUNK
    return pl.pallas_call(
        functools.partial(_attn_kernel, nb=nb, tq=tq),
        grid=(b, s // tq),
        in_specs=[pl.BlockSpec(memory_space=pltpu.SMEM),
                  pl.BlockSpec((1, tq, 512), lambda i, j: (i, j, 0)),
                  pl.BlockSpec((1, s, 512), lambda i, j: (i, 0, 1)),
                  pl.BlockSpec((N_ATT_HEADS, CHUNK, 3 * CHUNK), lambda i, j: (0, 0, 0))],
        out_specs=pl.BlockSpec((1, tq, 512), lambda i, j: (i, j, 0)),
        out_shape=jax.ShapeDtypeStruct((b, s, 512), BF16),
        compiler_params=_cparams("parallel", "parallel"),
        name="attn",
    )(sink, att, att, bias)


def _mlstm_kernel(mqk_ref, mv_ref, mo_ref, g4_ref, gb_ref, cw_ref, cb_ref, o_ref,
                  qk_scr, rrow_scr, rcol_scr, hf_scr, hb_scr, st_scr, m_scr, *, nc):
    L = CHUNK
    s_len = nc * L
    ng = 4 * N_ML_HEADS
    lane = lax.broadcasted_iota(jnp.int32, (1, LANES), 1)
    lo = lane < HEAD_DIM

    w0, w1, w2, cb = cw_ref[0:1, :], cw_ref[1:2, :], cw_ref[2:3, :], cb_ref[...]
    row = lax.broadcasted_iota(jnp.int32, (L, 1), 0)
    lane_qk = lax.broadcasted_iota(jnp.int32, (1, 4 * LANES), 1)
    kscale = jnp.where(lane_qk >= 2 * LANES, HEAD_DIM ** -0.5, 1.0).astype(F32)

    def conv_body(c, carry):
        r0 = pl.multiple_of(c * L, L)
        xc = mqk_ref[0, pl.ds(r0, L), :]
        pr = mqk_ref[0, pl.ds(jnp.maximum(r0 - 1, 0), 1), :]
        nx = mqk_ref[0, pl.ds(jnp.minimum(r0 + L, s_len - 1), 1), :]
        pr = jnp.where(c > 0, pr, 0.0)
        nx = jnp.where(c < nc - 1, nx, 0.0)
        xp = jnp.where(row == 0, pr, pltpu.roll(xc, 1, 0))
        xn = jnp.where(row == L - 1, nx, pltpu.roll(xc, L - 1, 0))
        y = w0 * xp + w1 * xc + w2 * xn + cb
        qk_scr[pl.ds(r0, L), :] = (_silu(y) * kscale).astype(BF16)
        return carry

    lax.fori_loop(0, nc, conv_body, 0)

    g = (g4_ref[0] + gb_ref[...]).reshape(nc * ng, L)
    kind = (lax.broadcasted_iota(jnp.int32, (nc * ng, 1), 0) % ng) // N_ML_HEADS
    logf = jnp.minimum(g, 0.0) - jnp.log1p(jnp.exp(-jnp.abs(g)))
    x = jnp.where(jnp.logical_or(kind == 1, kind == 3), logf, 0.0)
    iu = lax.broadcasted_iota(jnp.int32, (L, L), 0)
    it = lax.broadcasted_iota(jnp.int32, (L, L), 1)
    prefix = jnp.dot(x, (iu <= it).astype(F32), precision=HIGHEST, preferred_element_type=F32)
    suffix = jnp.dot(x, (iu >= it).astype(F32), precision=HIGHEST, preferred_element_type=F32)
    r = jnp.where(kind == 1, prefix, jnp.where(kind == 3, suffix, g))
    rrow_scr[...] = r.reshape(nc, ng, L)
    zpad = jnp.zeros((L - ng, L), F32)
    for c in range(nc):
        rcol_scr[c] = jnp.concatenate([r[c * ng:(c + 1) * ng], zpad], axis=0).T

    st_scr[...] = jnp.zeros(st_scr.shape, F32)
    m_scr[...] = jnp.zeros(m_scr.shape, F32)

    ones_blk = jnp.ones((L, LANES), BF16)

    def chunk_dir(c, dirn, h_scr):
        r0 = pl.multiple_of(c * L, L)
        qk = qk_scr[pl.ds(r0, L), :]
        vrows = mv_ref[0, pl.ds(r0, L), :]
        rr = rrow_scr[c]
        rc = rcol_scr[c]
        causal = (it <= iu) if dirn == 0 else (it >= iu)
        for grp in range(N_ML_HEADS // 2):
            qg = qk[:, grp * LANES:(grp + 1) * LANES]
            kg = qk[:, (2 + grp) * LANES:(3 + grp) * LANES]
            vg = vrows[:, grp * LANES:(grp + 1) * LANES]
            zb = jnp.zeros_like(qg)
            hv = []
            for odd in range(2):
                h = grp * 2 + odd
                hm = lo if odd == 0 else jnp.logical_not(lo)
                qh = jnp.where(hm, qg, zb)
                kh = jnp.where(hm, kg, zb)
                vext = jnp.concatenate([jnp.where(hm, vg, zb), ones_blk], axis=1)
                ii, bi = dirn * 2 * N_ML_HEADS + h, dirn * 2 * N_ML_HEADS + N_ML_HEADS + h
                i_row, b_row = rr[ii:ii + 1, :], rr[bi:bi + 1, :]
                i_col, b_col = rc[:, ii:ii + 1], rc[:, bi:bi + 1]
                b_last = b_row[:, L - 1:L] if dirn == 0 else b_row[:, 0:1]
                sidx = dirn * N_ML_HEADS + h
                m_old = m_scr[sidx][:, 0:1]
                st = st_scr[sidx]
                a_row = i_row - b_row
                a_max = jnp.max(jnp.where(causal, a_row, -jnp.inf), axis=1, keepdims=True)
                m_t = jnp.maximum(b_col + m_old, b_col + a_max)
                u = jnp.broadcast_to(b_col - m_t, (L, L))
                smat = (lax.dot_general(qh, kg, NT_DIMS, preferred_element_type=F32)
                        * jnp.exp(jnp.where(causal, u + a_row, -jnp.inf)))
                iw = jnp.concatenate([jnp.exp(u + m_old)] * 2, axis=1)
                num = (jnp.dot(smat.astype(BF16), vext, preferred_element_type=F32)
                       + iw * jnp.dot(qh, st.astype(BF16), preferred_element_type=F32))
                hv.append(num[:, :LANES] / jnp.maximum(jnp.abs(num[:, LANES:]), jnp.exp(-m_t)))
                ws_col = b_last - b_col + i_col
                ws_row = b_last - b_row + i_row
                m_new = jnp.maximum(b_last + m_old, jnp.max(ws_row, axis=1, keepdims=True))
                wv = (jnp.exp(ws_col - m_new) * vext.astype(F32)).astype(BF16)
                decay = jnp.exp(b_last + m_old - m_new)
                st_scr[sidx] = decay * st + lax.dot_general(kh, wv, TN_DIMS, preferred_element_type=F32)
                m_scr[sidx] = jnp.broadcast_to(m_new, (1, LANES))
            h_scr[pl.ds(r0, L), grp * LANES:(grp + 1) * LANES] = jnp.where(lo, hv[0], hv[1])

    def body(i, carry):
        chunk_dir(i, 0, hf_scr)
        chunk_dir(nc - 1 - i, 1, hb_scr)
        return carry

    lax.fori_loop(0, nc, body, 0)
    o_ref[0] = (jax.nn.sigmoid(mo_ref[0]) * (hf_scr[...] + hb_scr[...])).astype(o_ref.dtype)


def _mlstm(mqk, mv, mo, mg, gate_b, conv_w, conv_b):
    b, s, _ = mqk.shape
    nc = s // CHUNK
    ng = 4 * N_ML_HEADS
    g4 = mg[:, :, :ng].reshape(b, nc, CHUNK, ng).transpose(0, 1, 3, 2)
    seq = lambda w: pl.BlockSpec((1, s, w), lambda i: (i, 0, 0))
    return pl.pallas_call(
        functools.partial(_mlstm_kernel, nc=nc),
        grid=(b,),
        in_specs=[seq(512), seq(256), seq(256),
                  pl.BlockSpec((1, nc, ng, CHUNK), lambda i: (i, 0, 0, 0)),
                  pl.BlockSpec((ng, 1), lambda i: (0, 0)),
                  pl.BlockSpec((ML_CONV, 512), lambda i: (0, 0)),
                  pl.BlockSpec((1, 512), lambda i: (0, 0))],
        out_specs=seq(256),
        out_shape=jax.ShapeDtypeStruct((b, s, 256), BF16),
        scratch_shapes=[pltpu.VMEM((s, 512), BF16),
                        pltpu.VMEM((nc, ng, CHUNK), F32),
                        pltpu.VMEM((nc, CHUNK, LANES), F32),
                        pltpu.VMEM((s, 256), F32),
                        pltpu.VMEM((s, 256), F32),
                        pltpu.VMEM((2 * N_ML_HEADS, LANES, 2 * LANES), F32),
                        pltpu.VMEM((2 * N_ML_HEADS, 1, LANES), F32)],
        compiler_params=_cparams("parallel"),
        name="mlstm",
    )(mqk, mv, mo, g4, gate_b.reshape(ng, 1), conv_w, conv_b.reshape(1, 512))


def _gmlp_kernel(gg_ref, ws_ref, bs_ref, o_ref, *, tg):
    lane = lax.broadcasted_iota(jnp.int32, (1, 2 * LANES), 1)
    for j in range(tg // CHUNK):
        rows = slice(j * CHUNK, (j + 1) * CHUNK)
        u = _gelu(gg_ref[0, rows, 0:256])
        v = _ln(_gelu(gg_ref[0, rows, 256:512])).astype(BF16)
        zv = jnp.zeros_like(v)
        s = None
        for g in range(N_GM_GROUPS):
            ing = jnp.logical_and(lane >= g * HEAD_DIM, lane < (g + 1) * HEAD_DIM)
            part = jnp.dot(ws_ref[g], jnp.where(ing, v, zv), preferred_element_type=F32)
            part = part + jnp.where(ing, bs_ref[:, g:g + 1], 0.0)
            s = part if s is None else s + part
        o_ref[0, rows, :] = (u * s).astype(o_ref.dtype)


def _gmlp(gg, w_s, b_s):
    b, s, _ = gg.shape
    tg = 512
    return pl.pallas_call(
        functools.partial(_gmlp_kernel, tg=tg),
        grid=(b, s // tg),
        in_specs=[pl.BlockSpec((1, tg, 512), lambda i, j: (i, j, 0)),
                  pl.BlockSpec((N_GM_GROUPS, CHUNK, CHUNK), lambda i, j: (0, 0, 0)),
                  pl.BlockSpec((CHUNK, N_GM_GROUPS), lambda i, j: (0, 0))],
        out_specs=pl.BlockSpec((1, tg, 256), lambda i, j: (i, j, 0)),
        out_shape=jax.ShapeDtypeStruct((b, s, 256), BF16),
        compiler_params=_cparams("parallel", "parallel"),
        name="gmlp",
    )(gg, w_s.astype(BF16), b_s.T)


def _mixout_kernel(att_ref, ml_ref, gm_ref, x_ref, mod_ref, wo_ref, lng_ref, lnb_ref, wr_ref,
                   x1_ref, h2_ref, lg_ref, *, alpha):
    mix = jnp.dot(att_ref[0], wo_ref[0, 0:512, :], preferred_element_type=F32)
    mix = mix + jnp.dot(ml_ref[0], wo_ref[0, 512:768, :], preferred_element_type=F32)
    mix = mix + jnp.dot(gm_ref[0], wo_ref[0, 768:1024, :], preferred_element_type=F32)
    y = alpha * x_ref[0] + (1.0 + mod_ref[0, 0, 2:3, :]) * mix
    x1 = _ln(y) * lng_ref[...] + lnb_ref[...]
    x1_ref[0] = x1
    h2 = _ln(x1) * (1.0 + mod_ref[0, 0, 4:5, :]) + mod_ref[0, 0, 3:4, :]
    hi = h2.astype(BF16)
    h2_ref[0] = hi
    lo = (h2 - hi.astype(F32)).astype(BF16)
    lg_ref[0] = jnp.dot(jnp.concatenate([hi, lo, hi], axis=1), wr_ref[0], preferred_element_type=F32)


def _split_router(w_router):
    w = jnp.pad(w_router, ((0, 0), (0, 0), (0, LANES - w_router.shape[-1])))
    hi = w.astype(BF16)
    lo = (w - hi.astype(F32)).astype(BF16)
    return jnp.concatenate([hi, hi, lo], axis=1)


def _mixout(att, ml, gm, x, mod, w_out, ln_g, ln_b, w_router3, alpha, l):
    b, s, d = x.shape
    tm = 512
    row = lambda w: pl.BlockSpec((1, tm, w), lambda i, j: (i, j, 0))
    full = lambda a: pl.BlockSpec(a.shape, lambda i, j: (0,) * a.ndim)
    layer = lambda a: pl.BlockSpec((1,) + a.shape[1:], lambda i, j: (l,) + (0,) * (a.ndim - 1))
    return pl.pallas_call(
        functools.partial(_mixout_kernel, alpha=alpha),
        grid=(b, s // tm),
        in_specs=[row(512), row(256), row(256), row(d),
                  pl.BlockSpec((1, 1, 6, d), lambda i, j: (l, i, 0, 0)),
                  layer(w_out), full(ln_g), full(ln_b), layer(w_router3)],
        out_specs=[row(d), row(d), row(LANES)],
        out_shape=[jax.ShapeDtypeStruct((b, s, d), F32),
                   jax.ShapeDtypeStruct((b, s, d), BF16),
                   jax.ShapeDtypeStruct((b, s, LANES), F32)],
        compiler_params=_cparams("parallel", "parallel"),
        name="mixout",
    )(att, ml, gm, x, mod, w_out, ln_g, ln_b, w_router3)


def _route_kernel(lg_ref, pos_ref, aff_ref, *, cap, ne, nc):
    L = CHUNK
    lane = lax.broadcasted_iota(jnp.int32, (1, LANES), 1)
    l = jnp.where(lane < ne, lg_ref[0], -jnp.inf)
    e = jnp.exp(l - jnp.max(l, axis=1, keepdims=True))
    aff = e / jnp.sum(e, axis=1, keepdims=True)
    aff_ref[0] = aff
    bits = pltpu.bitcast(aff, jnp.int32)

    def bit_body(k, thr):
        cand = jnp.bitwise_or(thr, jnp.left_shift(jnp.int32(1), 30 - k))
        cnt = jnp.sum((bits >= cand).astype(jnp.int32), axis=0, keepdims=True)
        return jnp.where(cnt >= cap, cand, thr)

    thr = lax.fori_loop(0, 31, bit_body, jnp.zeros((1, LANES), jnp.int32))
    n_gt = jnp.sum((bits > thr).astype(jnp.int32), axis=0, keepdims=True)
    need = (cap - n_gt).astype(F32)

    it = lax.broadcasted_iota(jnp.int32, (L, L), 0)
    iu = lax.broadcasted_iota(jnp.int32, (L, L), 1)
    before = (iu < it).astype(BF16)

    run_eq = run_sel = jnp.zeros((1, LANES), F32)
    for c in range(nc):
        rows = slice(c * L, (c + 1) * L)
        bc = pltpu.bitcast(aff_ref[0, rows, :], jnp.int32)
        eq = bc == thr
        eqf = jnp.where(eq, 1.0, 0.0)
        eq_before = jnp.dot(before, eqf.astype(BF16), preferred_element_type=F32) + run_eq
        sel = jnp.logical_or(bc > thr, jnp.logical_and(eq, eq_before < need))
        self_ = jnp.where(sel, 1.0, 0.0)
        sel_before = jnp.dot(before, self_.astype(BF16), preferred_element_type=F32) + run_sel
        pos_ref[0, rows, :] = jnp.where(sel, sel_before.astype(jnp.int32), -1)
        run_eq = run_eq + jnp.sum(eqf, axis=0, keepdims=True)
        run_sel = run_sel + jnp.sum(self_, axis=0, keepdims=True)


def _route(logits, ne, cap):
    b, s, _ = logits.shape
    blk = pl.BlockSpec((1, s, LANES), lambda i: (i, 0, 0))
    return pl.pallas_call(
        functools.partial(_route_kernel, cap=cap, ne=ne, nc=s // CHUNK),
        grid=(b,),
        in_specs=[blk],
        out_specs=[blk, blk],
        out_shape=[jax.ShapeDtypeStruct((b, s, LANES), jnp.int32),
                   jax.ShapeDtypeStruct((b, s, LANES), F32)],
        compiler_params=_cparams("parallel"),
        name="route",
    )(logits)


def _ffn_kernel(pos_ref, aff_ref, h2_ref, wg_ref, wu_ref, wd_ref, o_ref, acc_scr, *, cap, tf):
    nseq, s = h2_ref.shape[0], h2_ref.shape[1]
    slot = lax.broadcasted_iota(jnp.int32, (cap, s), 0)
    xgs, gates = [], []
    for j in range(nseq):
        hit = pos_ref[j, 0] == slot
        onehot = jnp.where(hit, 1.0, 0.0).astype(BF16)
        gates.append(jnp.sum(jnp.where(hit, aff_ref[j, 0], 0.0), axis=1, keepdims=True))
        xgs.append(jnp.dot(onehot, h2_ref[j], preferred_element_type=F32).astype(BF16))
    xg = jnp.concatenate(xgs, axis=0)
    f = wg_ref.shape[3]
    for k in range(f // tf):
        cols = slice(k * tf, (k + 1) * tf)
        a = jnp.dot(xg, wg_ref[0, 0, :, cols], preferred_element_type=F32)
        u = jnp.dot(xg, wu_ref[0, 0, :, cols], preferred_element_type=F32)
        hid = (_silu(a) * u).astype(BF16)
        part = jnp.dot(hid, wd_ref[0, 0, cols, :], preferred_element_type=F32)
        if k == 0:
            acc_scr[...] = part
        else:
            acc_scr[...] += part
    for j in range(nseq):
        o_ref[j] = (acc_scr[j * cap:(j + 1) * cap, :] * gates[j]).astype(o_ref.dtype)


FFN_SEQS = 2


def _expert_ffn(pos_t, aff_t, h2, w_gate, w_up, w_down, cap, l):
    b, s, d = h2.shape
    _, ne, _, f = w_gate.shape
    nseq = FFN_SEQS if b % FFN_SEQS == 0 else 1
    return pl.pallas_call(
        functools.partial(_ffn_kernel, cap=cap, tf=1024),
        grid=(ne, b // nseq),
        in_specs=[pl.BlockSpec((nseq, 1, 1, s), lambda e, i: (i, e, 0, 0)),
                  pl.BlockSpec((nseq, 1, 1, s), lambda e, i: (i, e, 0, 0)),
                  pl.BlockSpec((nseq, s, d), lambda e, i: (i, 0, 0)),
                  pl.BlockSpec((1, 1, d, f), lambda e, i: (l, e, 0, 0)),
                  pl.BlockSpec((1, 1, d, f), lambda e, i: (l, e, 0, 0)),
                  pl.BlockSpec((1, 1, f, d), lambda e, i: (l, e, 0, 0))],
        out_specs=pl.BlockSpec((nseq, cap, d), lambda e, i: (i, e, 0)),
        out_shape=jax.ShapeDtypeStruct((b, ne * cap, d), BF16),
        scratch_shapes=[pltpu.VMEM((nseq * cap, d), F32)],
        compiler_params=_cparams("parallel", "parallel"),
        name="expert_ffn",
    )(pos_t, aff_t, h2, w_gate, w_up, w_down)


def _scatter_kernel(pos_ref, ye_ref, x1_ref, mod_ref, lng_ref, lnb_ref, o_ref, *, cap, ne, alpha):
    ts = pos_ref.shape[1]
    slot = lax.broadcasted_iota(jnp.int32, (ts, cap), 1)
    pos = pos_ref[0]
    onehot = jnp.concatenate(
        [jnp.where(pos[:, e:e + 1] == slot, 1.0, 0.0).astype(BF16) for e in range(ne)], axis=1)
    y = jnp.dot(onehot, ye_ref[0], preferred_element_type=F32)
    y = alpha * x1_ref[0] + (1.0 + mod_ref[0, 0, 5:6, :]) * y
    o_ref[0] = _ln(y) * lng_ref[...] + lnb_ref[...]


def _scatter(pos, ye, x1, mod, ln_g, ln_b, alpha, ne, l):
    b, s, d = x1.shape
    cap = ye.shape[1] // ne
    ts = 512
    return pl.pallas_call(
        functools.partial(_scatter_kernel, cap=cap, ne=ne, alpha=alpha),
        grid=(b, s // ts),
        in_specs=[pl.BlockSpec((1, ts, LANES), lambda i, j: (i, j, 0)),
                  pl.BlockSpec((1, ne * cap, d), lambda i, j: (i, 0, 0)),
                  pl.BlockSpec((1, ts, d), lambda i, j: (i, j, 0)),
                  pl.BlockSpec((1, 1, 6, d), lambda i, j: (l, i, 0, 0)),
                  pl.BlockSpec((1, d), lambda i, j: (0, 0)),
                  pl.BlockSpec((1, d), lambda i, j: (0, 0))],
        out_specs=pl.BlockSpec((1, ts, d), lambda i, j: (i, j, 0)),
        out_shape=jax.ShapeDtypeStruct((b, s, d), F32),
        compiler_params=_cparams("parallel", "parallel"),
        name="scatter_ln",
    )(pos, ye, x1, mod, ln_g, ln_b)


def kernel(x, c, w_ada, b_ada, w_in, conv_w, conv_b, gate_b, sink, rel_bias, w_s, b_s, w_out, w_router,
           w_gate, w_up, w_down, ln_g, ln_b):
    depth = w_ada.shape[0]
    b, s, d = x.shape
    ne = w_router.shape[-1]
    cap = max(1, min(s, EC_CAPACITY_FACTOR * s // ne))
    alpha = float((2 * depth) ** 0.25)

    mod = _modulation(c, w_ada, b_ada).reshape(depth, b, 6, d)
    bias = _band_bias(rel_bias)
    w_in_r = _rearrange_w_in(w_in)
    w_router3 = _split_router(w_router)
    w_out_h, w_gate_h, w_up_h, w_down_h = (w.astype(BF16) for w in (w_out, w_gate, w_up, w_down))

    for l in range(depth):
        att, mqk, mv, mo, gg, mg = _projection(x, mod, w_in_r, l)
        att_o = _attention(att, sink[l], bias)
        ml_o = _mlstm(mqk, mv, mo, mg, gate_b[l], conv_w[l], conv_b[l])
        gm_o = _gmlp(gg, w_s[l], b_s[l])
        x1, h2, logits = _mixout(att_o, ml_o, gm_o, x, mod, w_out_h, ln_g[l, 0:1], ln_b[l, 0:1],
                                 w_router3, alpha, l)
        pos, aff = _route(logits, ne, cap)
        pos_t = pos[:, :, :ne].transpose(0, 2, 1).reshape(b, ne, 1, s)
        aff_t = aff[:, :, :ne].transpose(0, 2, 1).reshape(b, ne, 1, s)
        ye = _expert_ffn(pos_t, aff_t, h2, w_gate_h, w_up_h, w_down_h, cap, l)
        x = _scatter(pos, ye, x1, mod, ln_g[l, 1:2], ln_b[l, 1:2], alpha, ne, l)
    return x
```
